```python
import math
import jax, jax.numpy as jnp
from jax import lax
import numpy as np

D_MODEL = 2048
BATCH = 4
SEQ = 2048
DEPTH = 1
DEC_BATCH = 128
DEC_SEQ = 8
PAST_LEN = 16384
PAGE_SIZE = 128

POOL_WINDOWS = (2, 4, 8, 16)
POOL_MAX = 16
D_POOL = D_MODEL // 2
POOL_GROUP = D_POOL // 4
POOL_OUT_GROUP = D_MODEL // 4
N_HEADS = 8
DK = 128
DV = 256
D_QK = N_HEADS * DK
D_V = N_HEADS * DV
RET_CHUNK = 128
ROPE_BASE = 10000.0
D_FF = 5632
CONV_K = 3
EPS = 1e-6

IN_SPLITS = (D_POOL, D_QK, D_QK, D_V, D_V, D_MODEL, D_MODEL)
N_IN = sum(IN_SPLITS)

kernel_name = "hybrid_pool_retention_convffn_step"


def _split_points(sizes):
    pts, acc = [], 0
    for s in sizes[:-1]:
        acc += s
        pts.append(acc)
    return pts


def rmsnorm(x, g):
    x32 = x.astype(jnp.float32)
    y = x32 * lax.rsqrt(jnp.mean(x32 * x32, axis=-1, keepdims=True) + EPS)
    return (y * g.astype(jnp.float32)).astype(x.dtype)


def rotary(x, pos):
    half = DK // 2
    theta = ROPE_BASE ** (-jnp.arange(half, dtype=jnp.float32) / half)
    ang = pos.astype(jnp.float32)[:, None] * theta[None, :]
    cos = jnp.cos(ang)[None, :, None, :]
    sin = jnp.sin(ang)[None, :, None, :]
    x32 = x.astype(jnp.float32)
    x1, x2 = x32[..., :half], x32[..., half:]
    return jnp.concatenate([x1 * cos - x2 * sin, x2 * cos + x1 * sin], axis=-1)


def pool_mix(u_ext, pos, w_pool, pool_scale):
    P = POOL_MAX
    L = u_ext.shape[1] - (P - 1)
    u32 = u_ext.astype(jnp.float32)
    cz = jnp.concatenate([jnp.zeros_like(u32[:, :1]), jnp.cumsum(u32, axis=1)], axis=1)
    u = u32[:, P - 1:]
    outs = []
    for g, w in enumerate(POOL_WINDOWS):
        sl = slice(g * POOL_GROUP, (g + 1) * POOL_GROUP)
        wsum = cz[:, P:P + L, sl] - cz[:, P - w:P - w + L, sl]
        cnt = jnp.minimum(w, pos + 1).astype(jnp.float32)[None, :, None]
        z = wsum / cnt - u[:, :, sl]
        outs.append(jnp.einsum('bld,de->ble', z, w_pool[g].astype(jnp.float32)))
    a = jnp.concatenate(outs, axis=-1) * pool_scale.astype(jnp.float32)
    return a.astype(u_ext.dtype)


def retention(q, k, v, state0):
    B, L = q.shape[0], q.shape[1]
    C = RET_CHUNK if L % RET_CHUNK == 0 else L
    n = L // C
    log_g = jnp.log(1.0 - 2.0 ** (-5.0 - jnp.arange(N_HEADS, dtype=jnp.float32)))
    idx = jnp.arange(C, dtype=jnp.float32)
    rel = idx[:, None] - idx[None, :]
    dmask = jnp.where(rel >= 0, jnp.exp(jnp.maximum(rel, 0.0)[None] * log_g[:, None, None]), 0.0)
    xi = jnp.exp((idx + 1.0)[None, :] * log_g[:, None])
    zeta = jnp.exp((C - 1.0 - idx)[None, :] * log_g[:, None])
    g_c = jnp.exp(C * log_g)

    def to_chunks(t):
        d = t.shape[-1]
        return t.reshape(B, n, C, N_HEADS, d).transpose(1, 0, 3, 2, 4)

    def step(R, inp):
        qc, kc, vc = inp
        s = jnp.einsum('bhid,bhjd->bhij', qc, kc) * dmask[None]
        o = jnp.einsum('bhij,bhjv->bhiv', s, vc) + \
            jnp.einsum('bhid,bhdv->bhiv', qc, R) * xi[None, :, :, None]
        R = R * g_c[None, :, None, None] + \
            jnp.einsum('bhjd,bhjv->bhdv', kc * zeta[None, :, :, None], vc)
        return R, o

    R, o = lax.scan(step, state0.astype(jnp.float32), (to_chunks(q), to_chunks(k), to_chunks(v)))
    o = o.transpose(1, 0, 3, 2, 4).reshape(B, L, N_HEADS, DV)
    return o, R


def causal_dwconv(u_ext, conv_w, conv_b):
    L = u_ext.shape[1] - (CONV_K - 1)
    y = conv_b[None, None, :]
    for j in range(CONV_K):
        y = y + u_ext[:, j:j + L] * conv_w[j][None, None, :]
    return y


def layer(x, pos, pool_buf, ret_state, conv_buf,
          g_pre_mix, w_in, w_pool, pool_scale, gn_gain, w_out, g_post_mix,
          g_pre_ffn, w_up, conv_w, conv_b, w_down, g_post_ffn):
    B, L, _ = x.shape
    h = rmsnorm(x, g_pre_mix)
    proj = jnp.einsum('bld,dn->bln', h, w_in)
    u_pool, q, k, v, g_ret, g_a, g_r = jnp.split(proj, _split_points(IN_SPLITS), axis=-1)

    pool_ext = jnp.concatenate([pool_buf.astype(u_pool.dtype), u_pool], axis=1)
    a = pool_mix(pool_ext, pos, w_pool, pool_scale)

    q = rotary(q.reshape(B, L, N_HEADS, DK), pos)
    k = rotary(k.reshape(B, L, N_HEADS, DK), pos) * (DK ** -0.5)
    v = v.reshape(B, L, N_HEADS, DV).astype(jnp.float32)
    o, R = retention(q, k, v, ret_state)
    mu = jnp.mean(o, axis=-1, keepdims=True)
    var = jnp.mean(jnp.square(o - mu), axis=-1, keepdims=True)
    o = ((o - mu) * lax.rsqrt(var + EPS)).reshape(B, L, D_V) * gn_gain.astype(jnp.float32)
    r = (jax.nn.silu(g_ret.astype(jnp.float32)) * o).astype(x.dtype)

    m = jax.nn.sigmoid(g_a) * a + jax.nn.sigmoid(g_r) * r
    x1 = x + rmsnorm(jnp.einsum('bld,de->ble', m, w_out), g_post_mix)

    h2 = rmsnorm(x1, g_pre_ffn)
    up = jnp.einsum('bld,df->blf', h2, w_up)
    up_ext = jnp.concatenate([conv_buf.astype(up.dtype), up], axis=1)
    c = causal_dwconv(up_ext, conv_w, conv_b)
    val, gate = c[..., :D_FF], c[..., D_FF:]
    f = jnp.einsum('blf,fd->bld', jax.nn.gelu(gate, approximate=True) * val, w_down)
    y = x1 + rmsnorm(f, g_post_ffn)

    new_pool = pool_ext[:, -(POOL_MAX - 1):]
    new_conv = up_ext[:, -(CONV_K - 1):]
    return y, new_pool, R.astype(x.dtype), new_conv


def setup_inputs(seed: int = 0) -> dict:
    key = jax.random.key(seed)
    ks = jax.random.split(key, 20)
    f32 = jnp.float32
    nrm = lambda k, s: jax.random.normal(k, s, f32)
    return {
        "x_prompt": nrm(ks[0], (BATCH, SEQ, D_MODEL)),
        "x_sample": nrm(ks[1], (DEC_BATCH, DEC_SEQ, D_MODEL)),
        "state_pool": nrm(ks[2], (DEC_BATCH, POOL_MAX - 1, D_POOL)),
        "state_ret": 0.1 * nrm(ks[3], (DEC_BATCH, N_HEADS, DK, DV)),
        "state_conv": nrm(ks[4], (DEC_BATCH, CONV_K - 1, 2 * D_FF)),
        "g_pre_mix": 1.0 + 0.02 * nrm(ks[5], (D_MODEL,)),
        "w_in": nrm(ks[6], (D_MODEL, N_IN)) * D_MODEL ** -0.5,
        "w_pool": nrm(ks[7], (4, POOL_GROUP, POOL_OUT_GROUP)) * POOL_GROUP ** -0.5,
        "pool_scale": 1.0 + 0.1 * nrm(ks[8], (D_MODEL,)),
        "gn_gain": 1.0 + 0.02 * nrm(ks[9], (D_V,)),
        "w_out": nrm(ks[10], (D_MODEL, D_MODEL)) * D_MODEL ** -0.5,
        "g_post_mix": 1.0 + 0.02 * nrm(ks[11], (D_MODEL,)),
        "g_pre_ffn": 1.0 + 0.02 * nrm(ks[12], (D_MODEL,)),
        "w_up": nrm(ks[13], (D_MODEL, 2 * D_FF)) * D_MODEL ** -0.5,
        "conv_w": nrm(ks[14], (CONV_K, 2 * D_FF)) * CONV_K ** -0.5,
        "conv_b": 0.01 * nrm(ks[15], (2 * D_FF,)),
        "w_down": nrm(ks[16], (D_FF, D_MODEL)) * D_FF ** -0.5,
        "g_post_ffn": 1.0 + 0.02 * nrm(ks[17], (D_MODEL,)),
    }


def reference(x_prompt, x_sample, state_pool, state_ret, state_conv,
              g_pre_mix, w_in, w_pool, pool_scale, gn_gain, w_out, g_post_mix,
              g_pre_ffn, w_up, conv_w, conv_b, w_down, g_post_ffn):
    weights = (g_pre_mix, w_in, w_pool, pool_scale, gn_gain, w_out, g_post_mix,
               g_pre_ffn, w_up, conv_w, conv_b, w_down, g_post_ffn)
    Bp, Lp = x_prompt.shape[0], x_prompt.shape[1]
    Ls = x_sample.shape[1]
    pos_p = jnp.arange(Lp, dtype=jnp.int32)
    pos_s = PAST_LEN + jnp.arange(Ls, dtype=jnp.int32)

    yp, sp_pool, sp_ret, sp_conv = x_prompt, None, None, None
    ys, ss_pool, ss_ret, ss_conv = x_sample, None, None, None
    for _ in range(DEPTH):
        yp, sp_pool, sp_ret, sp_conv = layer(
            yp, pos_p,
            jnp.zeros((Bp, POOL_MAX - 1, D_POOL), x_prompt.dtype),
            jnp.zeros((Bp, N_HEADS, DK, DV), x_prompt.dtype),
            jnp.zeros((Bp, CONV_K - 1, 2 * D_FF), x_prompt.dtype),
            *weights)
        ys, ss_pool, ss_ret, ss_conv = layer(
            ys, pos_s, state_pool, state_ret, state_conv, *weights)
    return (yp, ys, sp_pool, sp_ret, sp_conv, ss_pool, ss_ret, ss_conv)
```

```python
import functools

import jax
import jax.numpy as jnp
from jax import lax
from jax.experimental import pallas as pl
from jax.experimental.pallas import tpu as pltpu

F32 = jnp.float32
BF16 = jnp.bfloat16

EPS = 1e-6
N_HEADS = 8
DK = 128
DV = 256
POOL_WINDOWS = (2, 4, 8, 16)
POOL_MAX = 16
CONV_K = 3
ROPE_BASE = 10000.0
PAST_LEN = 16384
RET_CHUNK = 128

VMEM_LIMIT_BYTES = 56 * 1024 * 1024
ROW_TILE = 512
IN_COL_TILE = 1024
FF_COL_TILE = 512


def _params(*sem):
    return pltpu.CompilerParams(dimension_semantics=sem, vmem_limit_bytes=VMEM_LIMIT_BYTES)


def _rms(x, g):
    return x * lax.rsqrt(jnp.mean(x * x, axis=-1, keepdims=True) + EPS) * g


def _in_proj_kernel(x_ref, g_ref, cs_ref, w_ref, u_ref, p_ref, h_scr, *, kinds):
    j = pl.program_id(1)

    @pl.when(j == 0)
    def _():
        h_scr[...] = _rms(x_ref[...], g_ref[...]).astype(BF16)

    def proj():
        return jnp.dot(h_scr[...], w_ref[...], preferred_element_type=F32)

    def rotary(acc, scale):
        cos = cs_ref[:, :DK]
        sin = cs_ref[:, DK:]
        for h in range(acc.shape[1] // DK):
            xh = acc[:, h * DK:(h + 1) * DK]
            rot = xh * cos + pltpu.roll(xh, DK // 2, axis=1) * sin
            if scale != 1.0:
                rot = rot * scale
            p_ref[:, h * DK:(h + 1) * DK] = rot.astype(p_ref.dtype)

    def tiles_of(kind):
        return [t for t, k in enumerate(kinds) if k == kind]

    def cond_of(kind):
        ts = tiles_of(kind)
        return (j >= ts[0]) & (j <= ts[-1])

    @pl.when(cond_of("u"))
    def _():
        u_ref[...] = proj()

    @pl.when(cond_of("q"))
    def _():
        rotary(proj(), 1.0)

    @pl.when(cond_of("k"))
    def _():
        rotary(proj(), DK ** -0.5)

    @pl.when(cond_of("v"))
    def _():
        p_ref[...] = proj().astype(p_ref.dtype)

    @pl.when(cond_of("silu"))
    def _():
        acc = proj()
        p_ref[...] = (acc * jax.nn.sigmoid(acc)).astype(p_ref.dtype)

    @pl.when(cond_of("sig"))
    def _():
        p_ref[...] = jax.nn.sigmoid(proj()).astype(p_ref.dtype)


def _in_proj(x2, g, cs, w_in_b, *, d_pool, rest_dtype):
    m, d = x2.shape
    n_in = w_in_b.shape[1]
    tm, tn = ROW_TILE, IN_COL_TILE
    assert m % tm == 0 and n_in % tn == 0 and d_pool == tn and cs.shape[0] % tm == 0
    d_qk, d_v = N_HEADS * DK, N_HEADS * DV
    kinds = (["u"] * (d_pool // tn) + ["q"] * (d_qk // tn) + ["k"] * (d_qk // tn) + ["v"] * (d_v // tn)
             + ["silu"] * (d_v // tn) + ["sig"] * (2 * d // tn))
    assert len(kinds) == n_in // tn
    cs_blocks = cs.shape[0] // tm
    return pl.pallas_call(
        functools.partial(_in_proj_kernel, kinds=tuple(kinds)),
        grid=(m // tm, n_in // tn),
        in_specs=[
            pl.BlockSpec((tm, d), lambda i, j: (i, 0)),
            pl.BlockSpec((1, d), lambda i, j: (0, 0)),
            pl.BlockSpec((tm, 2 * DK), lambda i, j: (i % cs_blocks, 0)),
            pl.BlockSpec((d, tn), lambda i, j: (0, j)),
        ],
        out_specs=[
            pl.BlockSpec((tm, d_pool), lambda i, j: (i, 0)),
            pl.BlockSpec((tm, tn), lambda i, j: (i, jnp.maximum(j - d_pool // tn, 0))),
        ],
        out_shape=[
            jax.ShapeDtypeStruct((m, d_pool), F32),
            jax.ShapeDtypeStruct((m, n_in - d_pool), rest_dtype),
        ],
        scratch_shapes=[pltpu.VMEM((tm, d), BF16)],
        compiler_params=_params("arbitrary", "arbitrary"),
        name="in_proj",
    )(x2, g, cs, w_in_b)


def _group_norm(o):
    mu = jnp.mean(o, axis=-1, keepdims=True)
    d = o - mu
    var = jnp.mean(d * d, axis=-1, keepdims=True)
    return d * lax.rsqrt(var + EPS)


def _decay_tables(c):
    log_g = jnp.log(1.0 - 2.0 ** (-5.0 - jnp.arange(N_HEADS, dtype=F32)))
    idx = jnp.arange(c, dtype=F32)
    rel = idx[:, None] - idx[None, :]
    dmask = jnp.where(rel >= 0, jnp.exp(jnp.maximum(rel, 0.0)[None] * log_g[:, None, None]), 0.0)
    xi = jnp.exp((idx + 1.0)[None, :] * log_g[:, None])
    zeta = jnp.exp((c - 1.0 - idx)[None, :] * log_g[:, None])
    g_c = jnp.exp(c * log_g)
    xi_b = jnp.broadcast_to(xi[:, :, None], (N_HEADS, c, DV))
    zeta_b = jnp.broadcast_to(zeta[:, :, None], (N_HEADS, c, DK))
    gc_b = jnp.broadcast_to(g_c[:, None, None], (N_HEADS, 1, DV))
    return dmask, xi_b, zeta_b, gc_b


def _mixer_prompt_kernel(u_ref, q_ref, k_ref, v_ref, sr_ref, sa_ref, sg_ref,
                         wp_ref, ps_ref, gn_ref, dm_ref, xi_ref, zt_ref, gc_ref,
                         m_ref, rout_ref, ext_scr, r_scr):
    c = pl.program_id(1)
    rb = u_ref.shape[0]
    halo = POOL_MAX

    @pl.when(c == 0)
    def _():
        ext_scr[0:halo, :] = jnp.zeros((halo, ext_scr.shape[1]), F32)
        r_scr[...] = jnp.zeros(r_scr.shape, F32)

    ext_scr[halo:halo + rb, :] = u_ref[...]

    pg = u_ref.shape[1] // len(POOL_WINDOWS)
    pos = c * rb + lax.broadcasted_iota(jnp.int32, (rb, pg), 0)
    a_groups = []
    for g, w in enumerate(POOL_WINDOWS):
        cols = slice(g * pg, (g + 1) * pg)
        wsum = ext_scr[halo:halo + rb, cols]
        for lag in range(1, w):
            wsum = wsum + ext_scr[halo - lag:halo - lag + rb, cols]
        cnt = jnp.minimum(w, pos + 1).astype(F32)
        z = wsum / cnt - ext_scr[halo:halo + rb, cols]
        a_groups.append(jnp.dot(z.astype(BF16), wp_ref[g], preferred_element_type=F32))
    og = a_groups[0].shape[1]

    ext_scr[0:halo, :] = ext_scr[rb:rb + halo, :]

    for h in range(N_HEADS):
        qh = q_ref[:, h * DK:(h + 1) * DK]
        kh = k_ref[:, h * DK:(h + 1) * DK]
        vh = v_ref[:, h * DV:(h + 1) * DV]
        r_old = r_scr[h]
        s = lax.dot_general(qh, kh, (((1,), (1,)), ((), ())), preferred_element_type=F32) * dm_ref[h]
        inter = jnp.dot(qh, r_old.astype(BF16), preferred_element_type=F32) * xi_ref[h]
        o = jnp.dot(s.astype(BF16), vh, preferred_element_type=F32) + inter
        kz = (kh.astype(F32) * zt_ref[h]).astype(BF16)
        r_scr[h] = r_old * gc_ref[h] + lax.dot_general(
            kz, vh, (((0,), (0,)), ((), ())), preferred_element_type=F32)

        hc = slice(h * DV, (h + 1) * DV)
        on = _group_norm(o) * gn_ref[:, hc]
        r = sr_ref[:, hc].astype(F32) * on
        g, part = divmod(h * DV, og)
        a = a_groups[g][:, part:part + DV] * ps_ref[:, hc]
        mh = sa_ref[:, hc].astype(F32) * a + sg_ref[:, hc].astype(F32) * r
        m_ref[:, hc] = mh.astype(m_ref.dtype)

    @pl.when(c == pl.num_programs(1) - 1)
    def _():
        rout_ref[0] = r_scr[...]


def _mixer_prompt(u, rest, w_pool_b, pool_scale, gn_gain, *, batch, seq):
    m_rows, d_pool = u.shape
    d_v = N_HEADS * DV
    rb = RET_CHUNK
    assert seq % rb == 0 and m_rows == batch * seq
    nc = seq // rb
    dmask, xi_b, zeta_b, gc_b = _decay_tables(rb)
    d_qk = N_HEADS * DK
    row = lambda b, c: b * nc + c
    full = lambda shape: pl.BlockSpec(shape, lambda b, c: (0,) * len(shape))
    return pl.pallas_call(
        _mixer_prompt_kernel,
        grid=(batch, nc),
        in_specs=[
            pl.BlockSpec((rb, d_pool), lambda b, c: (row(b, c), 0)),
            pl.BlockSpec((rb, d_qk), lambda b, c: (row(b, c), 0)),
            pl.BlockSpec((rb, d_qk), lambda b, c: (row(b, c), 1)),
            pl.BlockSpec((rb, d_v), lambda b, c: (row(b, c), 1)),
            pl.BlockSpec((rb, d_v), lambda b, c: (row(b, c), 2)),
            pl.BlockSpec((rb, d_v), lambda b, c: (row(b, c), 3)),
            pl.BlockSpec((rb, d_v), lambda b, c: (row(b, c), 4)),
            full(w_pool_b.shape), full(pool_scale.shape), full(gn_gain.shape),
            full(dmask.shape), full(xi_b.shape), full(zeta_b.shape), full(gc_b.shape),
        ],
        out_specs=[
            pl.BlockSpec((rb, d_v), lambda b, c: (row(b, c), 0)),
            pl.BlockSpec((1, N_HEADS, DK, DV), lambda b, c: (b, 0, 0, 0)),
        ],
        out_shape=[
            jax.ShapeDtypeStruct((m_rows, d_v), BF16),
            jax.ShapeDtypeStruct((batch, N_HEADS, DK, DV), F32),
        ],
        scratch_shapes=[pltpu.VMEM((rb + POOL_MAX, d_pool), F32), pltpu.VMEM((N_HEADS, DK, DV), F32)],
        compiler_params=_params("arbitrary", "arbitrary"),
        name="mixer_prompt",
    )(u, rest, rest, rest, rest, rest, rest, w_pool_b, pool_scale, gn_gain, dmask, xi_b, zeta_b, gc_b)


def _mixer_sample_kernel(u_ref, sp_ref, q_ref, k_ref, v_ref, sr_ref, sa_ref, sg_ref, rin_ref,
                         wp_ref, ps_ref, gn_ref, dm_ref, xi_ref, zt_ref, gc_ref,
                         m_ref, rout_ref, ext_scr, *, pos0):
    bb, ln, d_pool = u_ref.shape
    hist = POOL_MAX - 1
    ext_scr[:, 1:1 + hist, :] = sp_ref[...]
    ext_scr[:, POOL_MAX:POOL_MAX + ln, :] = u_ref[...]

    pg = d_pool // len(POOL_WINDOWS)
    pos = pos0 + lax.broadcasted_iota(jnp.int32, (bb, ln, pg), 1)
    a_groups = []
    for g, w in enumerate(POOL_WINDOWS):
        cols = slice(g * pg, (g + 1) * pg)
        wsum = ext_scr[:, POOL_MAX:POOL_MAX + ln, cols]
        for lag in range(1, w):
            wsum = wsum + ext_scr[:, POOL_MAX - lag:POOL_MAX - lag + ln, cols]
        cnt = jnp.minimum(w, pos + 1).astype(F32)
        z = wsum / cnt - ext_scr[:, POOL_MAX:POOL_MAX + ln, cols]
        ag = jnp.dot(z.reshape(bb * ln, pg).astype(BF16), wp_ref[g], preferred_element_type=F32)
        a_groups.append(ag.reshape(bb, ln, ag.shape[-1]))
    og = a_groups[0].shape[-1]

    for h in range(N_HEADS):
        qh = q_ref[:, :, h * DK:(h + 1) * DK].astype(BF16)
        kf = k_ref[:, :, h * DK:(h + 1) * DK].astype(F32)
        kh = kf.astype(BF16)
        vh = v_ref[:, :, h * DV:(h + 1) * DV].astype(BF16)
        r_old = rin_ref[:, h]
        s = jnp.einsum("btd,bsd->bts", qh, kh, preferred_element_type=F32) * dm_ref[h]
        inter = jnp.einsum("btd,bdv->btv", qh, r_old.astype(BF16), preferred_element_type=F32) * xi_ref[h]
        o = jnp.einsum("bts,bsv->btv", s.astype(BF16), vh, preferred_element_type=F32) + inter
        kz_t = jnp.swapaxes(kf * zt_ref[h], 1, 2).astype(BF16)
        rout_ref[:, h] = r_old * gc_ref[h] + jnp.einsum(
            "bds,bsv->bdv", kz_t, vh, preferred_element_type=F32)

        hc = slice(h * DV, (h + 1) * DV)
        on = _group_norm(o) * gn_ref[:, hc]
        r = sr_ref[:, :, hc].astype(F32) * on
        g, part = divmod(h * DV, og)
        a = a_groups[g][:, :, part:part + DV] * ps_ref[:, hc]
        mh = sa_ref[:, :, hc].astype(F32) * a + sg_ref[:, :, hc].astype(F32) * r
        m_ref[:, :, hc] = mh.astype(m_ref.dtype)


def _mixer_sample(u, rest, state_pool, state_ret, w_pool_b, pool_scale, gn_gain, *, batch, seq, pos0, bb):
    d_pool = u.shape[-1]
    d_v, d_qk = N_HEADS * DV, N_HEADS * DK
    assert batch % bb == 0
    dmask, xi_b, zeta_b, gc_b = _decay_tables(seq)
    u3 = u.reshape(batch, seq, d_pool)
    rest3 = rest.reshape(batch, seq, rest.shape[-1])
    full = lambda shape: pl.BlockSpec(shape, lambda b: (0,) * len(shape))
    m3, r_new = pl.pallas_call(
        functools.partial(_mixer_sample_kernel, pos0=pos0),
        grid=(batch // bb,),
        in_specs=[
            pl.BlockSpec((bb, seq, d_pool), lambda b: (b, 0, 0)),
            pl.BlockSpec((bb, POOL_MAX - 1, d_pool), lambda b: (b, 0, 0)),
            pl.BlockSpec((bb, seq, d_qk), lambda b: (b, 0, 0)),
            pl.BlockSpec((bb, seq, d_qk), lambda b: (b, 0, 1)),
            pl.BlockSpec((bb, seq, d_v), lambda b: (b, 0, 1)),
            pl.BlockSpec((bb, seq, d_v), lambda b: (b, 0, 2)),
            pl.BlockSpec((bb, seq, d_v), lambda b: (b, 0, 3)),
            pl.BlockSpec((bb, seq, d_v), lambda b: (b, 0, 4)),
            pl.BlockSpec((bb, N_HEADS, DK, DV), lambda b: (b, 0, 0, 0)),
            full(w_pool_b.shape), full(pool_scale.shape), full(gn_gain.shape),
            full(dmask.shape), full(xi_b.shape), full(zeta_b.shape), full(gc_b.shape),
        ],
        out_specs=[
            pl.BlockSpec((bb, seq, d_v), lambda b: (b, 0, 0)),
            pl.BlockSpec((bb, N_HEADS, DK, DV), lambda b: (b, 0, 0, 0)),
        ],
        out_shape=[
            jax.ShapeDtypeStruct((batch, seq, d_v), F32),
            jax.ShapeDtypeStruct((batch, N_HEADS, DK, DV), F32),
        ],
        scratch_shapes=[pltpu.VMEM((bb, POOL_MAX + seq, d_pool), F32)],
        compiler_params=_params("arbitrary"),
        name="mixer_sample",
    )(u3, state_pool, rest3, rest3, rest3, rest3, rest3, rest3, state_ret,
      w_pool_b, pool_scale, gn_gain, dmask, xi_b, zeta_b, gc_b)
    return m3.reshape(batch * seq, d_v), r_new


def _out_proj_kernel(m_ref, w_ref, x_ref, gpost_ref, gpre_ref, x1_ref, h2_ref):
    proj = jnp.dot(m_ref[...].astype(BF16), w_ref[...], preferred_element_type=F32)
    x1 = x_ref[...] + _rms(proj, gpost_ref[...])
    x1_ref[...] = x1
    h2_ref[...] = _rms(x1, gpre_ref[...]).astype(BF16)


def _out_proj(m2, w_out_b, x2, g_post, g_pre):
    m, d = x2.shape
    tm = ROW_TILE
    assert m % tm == 0
    return pl.pallas_call(
        _out_proj_kernel,
        grid=(m // tm,),
        in_specs=[
            pl.BlockSpec((tm, m2.shape[1]), lambda i: (i, 0)),
            pl.BlockSpec(w_out_b.shape, lambda i: (0, 0)),
            pl.BlockSpec((tm, d), lambda i: (i, 0)),
            pl.BlockSpec((1, d), lambda i: (0, 0)),
            pl.BlockSpec((1, d), lambda i: (0, 0)),
        ],
        out_specs=[
            pl.BlockSpec((tm, d), lambda i: (i, 0)),
            pl.BlockSpec((tm, d), lambda i: (i, 0)),
        ],
        out_shape=[jax.ShapeDtypeStruct((m, d), F32), jax.ShapeDtypeStruct((m, d), BF16)],
        compiler_params=_params("arbitrary"),
        name="out_proj",
    )(m2, w_out_b, x2, g_post, g_pre)


def _gelu_tanh(x):
    return 0.5 * x * (1.0 + jnp.tanh(0.7978845608028654 * (x + 0.044715 * (x * x * x))))


def _conv_ffn_kernel(*refs, seq, blocks_per_seq):
    if seq >= ROW_TILE:
        (h2_ref, wv_ref, wg_ref, cwv_ref, cwg_ref, cbv_ref, cbg_ref, wd_ref, x1_ref, g_ref,
         y_ref, tail_ref, acc_scr, ext_scr, carry_scr) = refs
    else:
        (h2_ref, wv_ref, wg_ref, cwv_ref, cwg_ref, cbv_ref, cbg_ref, wd_ref, x1_ref, g_ref, stv_ref, stg_ref,
         y_ref, tail_ref, acc_scr) = refs
    i = pl.program_id(0)
    j = pl.program_id(1)
    tm = h2_ref.shape[0]
    hist = CONV_K - 1

    def conv_long(part, up, cw_ref, cb_ref):
        @pl.when(i % blocks_per_seq == 0)
        def _():
            ext_scr[part, 0:8, :] = jnp.zeros((8, up.shape[1]), F32)

        @pl.when(i % blocks_per_seq != 0)
        def _():
            ext_scr[part, 0:8, :] = carry_scr[j, part]

        ext_scr[part, 8:8 + tm, :] = up
        out = up * cw_ref[hist:hist + 1, :] + cb_ref[...]
        for lag in range(1, CONV_K):
            out = out + ext_scr[part, 8 - lag:8 - lag + tm, :] * cw_ref[hist - lag:hist - lag + 1, :]
        last = ext_scr[part, tm:tm + 8, :]
        carry_scr[j, part] = last
        tail_ref[0, part] = last
        return out

    def conv_short(part, up, cw_ref, cb_ref, st_ref):
        ns = tm // seq
        up3 = up.reshape(ns, seq, up.shape[1])
        t = lax.broadcasted_iota(jnp.int32, up3.shape, 1)
        out = up3 * cw_ref[hist:hist + 1, :] + cb_ref[...]
        for lag in range(1, CONV_K):
            shifted = pltpu.roll(up3, lag, axis=1)
            for r in range(lag):
                row = st_ref[:, hist - lag + r:hist - lag + r + 1, :]
                shifted = jnp.where(t == r, row, shifted)
            out = out + shifted * cw_ref[hist - lag:hist - lag + 1, :]
        tail_ref[:, part] = up3[:, seq - hist:, :]
        return out.reshape(tm, up.shape[1])

    h2 = h2_ref[...]
    upv = jnp.dot(h2, wv_ref[...], preferred_element_type=F32)
    upg = jnp.dot(h2, wg_ref[...], preferred_element_type=F32)
    if seq >= ROW_TILE:
        val = conv_long(0, upv, cwv_ref, cbv_ref)
        gate = conv_long(1, upg, cwg_ref, cbg_ref)
    else:
        val = conv_short(0, upv, cwv_ref, cbv_ref, stv_ref)
        gate = conv_short(1, upg, cwg_ref, cbg_ref, stg_ref)
    act = (_gelu_tanh(gate) * val).astype(BF16)
    contrib = jnp.dot(act, wd_ref[...], preferred_element_type=F32)

    @pl.when(j == 0)
    def _():
        acc_scr[...] = contrib

    @pl.when(j != 0)
    def _():
        acc_scr[...] += contrib

    @pl.when(j == pl.num_programs(1) - 1)
    def _():
        y_ref[...] = x1_ref[...] + _rms(acc_scr[...], g_ref[...])


def _conv_ffn(h2, x1, w_up_b, conv_w, conv_b, w_down_b, g_post, state_conv, *, seq):
    m, d = x1.shape
    d_ff = w_down_b.shape[0]
    tm, tf = ROW_TILE, FF_COL_TILE
    assert m % tm == 0 and d_ff % tf == 0
    nj = d_ff // tf
    hist = CONV_K - 1
    conv_b2 = conv_b.reshape(1, 2 * d_ff)
    long_seq = seq >= tm
    in_specs = [
        pl.BlockSpec((tm, d), lambda i, j: (i, 0)),
        pl.BlockSpec((d, tf), lambda i, j: (0, j)),
        pl.BlockSpec((d, tf), lambda i, j: (0, j + nj)),
        pl.BlockSpec((CONV_K, tf), lambda i, j: (0, j)),
        pl.BlockSpec((CONV_K, tf), lambda i, j: (0, j + nj)),
        pl.BlockSpec((1, tf), lambda i, j: (0, j)),
        pl.BlockSpec((1, tf), lambda i, j: (0, j + nj)),
        pl.BlockSpec((tf, d), lambda i, j: (j, 0)),
        pl.BlockSpec((tm, d), lambda i, j: (i, 0)),
        pl.BlockSpec((1, d), lambda i, j: (0, 0)),
    ]
    args = [h2, w_up_b, w_up_b, conv_w, conv_w, conv_b2, conv_b2, w_down_b, x1, g_post]
    scratch = [pltpu.VMEM((tm, d), F32)]
    if long_seq:
        assert seq % tm == 0
        blocks_per_seq = seq // tm
        tail_shape = (m // tm, 2, 8, d_ff)
        tail_spec = pl.BlockSpec((1, 2, 8, tf), lambda i, j: (i, 0, 0, j))
        scratch += [pltpu.VMEM((2, tm + 8, tf), F32), pltpu.VMEM((nj, 2, 8, tf), F32)]
    else:
        assert tm % seq == 0 and seq == 8 and seq >= hist
        blocks_per_seq = 0
        ns = tm // seq
        in_specs += [
            pl.BlockSpec((ns, hist, tf), lambda i, j: (i, 0, j)),
            pl.BlockSpec((ns, hist, tf), lambda i, j: (i, 0, j + nj)),
        ]
        args += [state_conv, state_conv]
        tail_shape = (m // seq, 2, hist, d_ff)
        tail_spec = pl.BlockSpec((ns, 2, hist, tf), lambda i, j: (i, 0, 0, j))
    y, tail = pl.pallas_call(
        functools.partial(_conv_ffn_kernel, seq=seq, blocks_per_seq=blocks_per_seq),
        grid=(m // tm, nj),
        in_specs=in_specs,
        out_specs=[pl.BlockSpec((tm, d), lambda i, j: (i, 0)), tail_spec],
        out_shape=[jax.ShapeDtypeStruct((m, d), F32), jax.ShapeDtypeStruct(tail_shape, F32)],
        scratch_shapes=scratch,
        compiler_params=_params("arbitrary", "arbitrary"),
        name="conv_ffn",
    )(*args)
    if long_seq:
        tail = tail[blocks_per_seq - 1::blocks_per_seq, :, 8 - hist:, :]
    nseq = tail.shape[0]
    new_conv = jnp.swapaxes(tail, 1, 2).reshape(nseq, hist, 2 * d_ff)
    return y, new_conv


def _rotary_table(pos, rows):
    half = DK // 2
    theta = ROPE_BASE ** (-jnp.arange(half, dtype=F32) / half)
    ang = pos.astype(F32)[:, None] * theta[None, :]
    cos, sin = jnp.cos(ang), jnp.sin(ang)
    cs = jnp.concatenate([cos, cos, -sin, sin], axis=-1)
    reps = max(1, rows // cs.shape[0])
    return jnp.tile(cs, (reps, 1))


def kernel(x_prompt, x_sample, state_pool, state_ret, state_conv, g_pre_mix, w_in, w_pool, pool_scale, gn_gain,
           w_out, g_post_mix, g_pre_ffn, w_up, conv_w, conv_b, w_down, g_post_ffn):
    d = x_prompt.shape[-1]
    d_pool = state_pool.shape[-1]
    row = lambda v: v.reshape(1, -1)
    w_in_b, w_pool_b, w_out_b = w_in.astype(BF16), w_pool.astype(BF16), w_out.astype(BF16)
    w_up_b, w_down_b = w_up.astype(BF16), w_down.astype(BF16)

    def dense_tail(x2, m2, state_conv_path, seq):
        x1, h2 = _out_proj(m2, w_out_b, x2, row(g_post_mix), row(g_pre_ffn))
        return _conv_ffn(h2, x1, w_up_b, conv_w, conv_b, w_down_b, row(g_post_ffn), state_conv_path, seq=seq)

    bp, lp, _ = x_prompt.shape
    xp = x_prompt.reshape(bp * lp, d)
    cs_p = _rotary_table(jnp.arange(lp, dtype=jnp.int32), ROW_TILE)
    u_p, rest_p = _in_proj(xp, row(g_pre_mix), cs_p, w_in_b, d_pool=d_pool, rest_dtype=BF16)
    m_p, ret_p = _mixer_prompt(u_p, rest_p, w_pool_b, row(pool_scale), row(gn_gain), batch=bp, seq=lp)
    y_p, conv_p = dense_tail(xp, m_p, None, lp)
    pool_p = u_p.reshape(bp, lp, d_pool)[:, lp - (POOL_MAX - 1):]

    bs, ls, _ = x_sample.shape
    xs = x_sample.reshape(bs * ls, d)
    cs_s = _rotary_table(PAST_LEN + jnp.arange(ls, dtype=jnp.int32), ROW_TILE)
    u_s, rest_s = _in_proj(xs, row(g_pre_mix), cs_s, w_in_b, d_pool=d_pool, rest_dtype=F32)
    m_s, ret_s = _mixer_sample(u_s, rest_s, state_pool, state_ret, w_pool_b, row(pool_scale), row(gn_gain),
                               batch=bs, seq=ls, pos0=PAST_LEN, bb=4)
    y_s, conv_s = dense_tail(xs, m_s, state_conv, ls)
    pool_s = jnp.concatenate([state_pool, u_s.reshape(bs, ls, d_pool)], axis=1)[:, -(POOL_MAX - 1):]

    return (y_p.reshape(bp, lp, d), y_s.reshape(bs, ls, d), pool_p, ret_p.astype(x_prompt.dtype), conv_p,
            pool_s, ret_s.astype(x_sample.dtype), conv_s)
```

```python
import functools

import jax
import jax.numpy as jnp
from jax import lax
from jax.experimental import pallas as pl
from jax.experimental.pallas import tpu as pltpu

F32 = jnp.float32
BF16 = jnp.bfloat16

EPS = 1e-6
N_HEADS = 8
DK = 128
DV = 256
POOL_WINDOWS = (2, 4, 8, 16)
POOL_MAX = 16
CONV_K = 3
ROPE_BASE = 10000.0
PAST_LEN = 16384
RET_CHUNK = 128

VMEM_LIMIT_BYTES = 56 * 1024 * 1024
ROW_TILE = 512
IN_ROW_TILE = 1024
IN_COL_TILE = 1024
FF_COL_TILE = 512


def _params(*sem):
    return pltpu.CompilerParams(dimension_semantics=sem, vmem_limit_bytes=VMEM_LIMIT_BYTES)


def _rms(x, g):
    return x * lax.rsqrt(jnp.mean(x * x, axis=-1, keepdims=True) + EPS) * g


def _in_proj_kernel(x_ref, g_ref, cs_ref, w_ref, u_ref, p_ref, h_scr, *, kinds):
    j = pl.program_id(1)

    @pl.when(j == 0)
    def _():
        h_scr[...] = _rms(x_ref[...], g_ref[...]).astype(BF16)

    def proj():
        return jnp.dot(h_scr[...], w_ref[...], preferred_element_type=F32)

    def rotary(acc, scale):
        cos = cs_ref[:, :DK]
        sin = cs_ref[:, DK:]
        for h in range(acc.shape[1] // DK):
            xh = acc[:, h * DK:(h + 1) * DK]
            rot = xh * cos + pltpu.roll(xh, DK // 2, axis=1) * sin
            if scale != 1.0:
                rot = rot * scale
            p_ref[:, h * DK:(h + 1) * DK] = rot.astype(p_ref.dtype)

    def tiles_of(kind):
        return [t for t, k in enumerate(kinds) if k == kind]

    def cond_of(kind):
        ts = tiles_of(kind)
        return (j >= ts[0]) & (j <= ts[-1])

    @pl.when(cond_of("u"))
    def _():
        u_ref[...] = proj()

    @pl.when(cond_of("q"))
    def _():
        rotary(proj(), 1.0)

    @pl.when(cond_of("k"))
    def _():
        rotary(proj(), DK ** -0.5)

    @pl.when(cond_of("v"))
    def _():
        p_ref[...] = proj().astype(p_ref.dtype)

    @pl.when(cond_of("silu"))
    def _():
        acc = proj()
        p_ref[...] = (acc * jax.nn.sigmoid(acc)).astype(p_ref.dtype)

    @pl.when(cond_of("sig"))
    def _():
        p_ref[...] = jax.nn.sigmoid(proj()).astype(p_ref.dtype)


def _in_proj(x2, g, cs, w_in_b, *, d_pool, rest_dtype):
    m, d = x2.shape
    n_in = w_in_b.shape[1]
    tm, tn = IN_ROW_TILE, IN_COL_TILE
    assert m % tm == 0 and n_in % tn == 0 and d_pool == tn and cs.shape[0] % tm == 0
    d_qk, d_v = N_HEADS * DK, N_HEADS * DV
    kinds = (["u"] * (d_pool // tn) + ["q"] * (d_qk // tn) + ["k"] * (d_qk // tn) + ["v"] * (d_v // tn)
             + ["silu"] * (d_v // tn) + ["sig"] * (2 * d // tn))
    assert len(kinds) == n_in // tn
    cs_blocks = cs.shape[0] // tm
    return pl.pallas_call(
        functools.partial(_in_proj_kernel, kinds=tuple(kinds)),
        grid=(m // tm, n_in // tn),
        in_specs=[
            pl.BlockSpec((tm, d), lambda i, j: (i, 0)),
            pl.BlockSpec((1, d), lambda i, j: (0, 0)),
            pl.BlockSpec((tm, 2 * DK), lambda i, j: (i % cs_blocks, 0)),
            pl.BlockSpec((d, tn), lambda i, j: (0, j)),
        ],
        out_specs=[
            pl.BlockSpec((tm, d_pool), lambda i, j: (i, 0)),
            pl.BlockSpec((tm, tn), lambda i, j: (i, jnp.maximum(j - d_pool // tn, 0))),
        ],
        out_shape=[
            jax.ShapeDtypeStruct((m, d_pool), F32),
            jax.ShapeDtypeStruct((m, n_in - d_pool), rest_dtype),
        ],
        scratch_shapes=[pltpu.VMEM((tm, d), BF16)],
        compiler_params=_params("arbitrary", "arbitrary"),
        name="in_proj",
    )(x2, g, cs, w_in_b)


def _group_norm(o):
    mu = jnp.mean(o, axis=-1, keepdims=True)
    d = o - mu
    var = jnp.mean(d * d, axis=-1, keepdims=True)
    return d * lax.rsqrt(var + EPS)


def _decay_tables(c):
    log_g = jnp.log(1.0 - 2.0 ** (-5.0 - jnp.arange(N_HEADS, dtype=F32)))
    idx = jnp.arange(c, dtype=F32)
    rel = idx[:, None] - idx[None, :]
    dmask = jnp.where(rel >= 0, jnp.exp(jnp.maximum(rel, 0.0)[None] * log_g[:, None, None]), 0.0)
    xi = jnp.exp((idx + 1.0)[None, :] * log_g[:, None])
    zeta = jnp.exp((c - 1.0 - idx)[None, :] * log_g[:, None])
    g_c = jnp.exp(c * log_g)
    xi_b = jnp.broadcast_to(xi[:, :, None], (N_HEADS, c, DV))
    zeta_b = jnp.broadcast_to(zeta[:, :, None], (N_HEADS, c, DK))
    gc_b = jnp.broadcast_to(g_c[:, None, None], (N_HEADS, 1, DV))
    return dmask, xi_b, zeta_b, gc_b


def _mixer_prompt_kernel(u_ref, q_ref, k_ref, v_ref, sr_ref, sa_ref, sg_ref,
                         wp_ref, ps_ref, gn_ref, dm_ref, xi_ref, zt_ref, gc_ref,
                         m_ref, rout_ref, ext_scr, r_scr):
    c = pl.program_id(1)
    rb = u_ref.shape[0]
    halo = POOL_MAX

    @pl.when(c == 0)
    def _():
        ext_scr[0:halo, :] = jnp.zeros((halo, ext_scr.shape[1]), F32)
        r_scr[...] = jnp.zeros(r_scr.shape, F32)

    ext_scr[halo:halo + rb, :] = u_ref[...]

    pg = u_ref.shape[1] // len(POOL_WINDOWS)
    pos = c * rb + lax.broadcasted_iota(jnp.int32, (rb, pg), 0)
    a_groups = []
    for g, w in enumerate(POOL_WINDOWS):
        cols = slice(g * pg, (g + 1) * pg)
        wsum = ext_scr[halo:halo + rb, cols]
        for lag in range(1, w):
            wsum = wsum + ext_scr[halo - lag:halo - lag + rb, cols]
        cnt = jnp.minimum(w, pos + 1).astype(F32)
        z = wsum / cnt - ext_scr[halo:halo + rb, cols]
        a_groups.append(jnp.dot(z.astype(BF16), wp_ref[g], preferred_element_type=F32))
    og = a_groups[0].shape[1]

    ext_scr[0:halo, :] = ext_scr[rb:rb + halo, :]

    for h in range(N_HEADS):
        qh = q_ref[:, h * DK:(h + 1) * DK]
        kh = k_ref[:, h * DK:(h + 1) * DK]
        vh = v_ref[:, h * DV:(h + 1) * DV]
        r_old = r_scr[h]
        s = lax.dot_general(qh, kh, (((1,), (1,)), ((), ())), preferred_element_type=F32) * dm_ref[h]
        inter = jnp.dot(qh, r_old.astype(BF16), preferred_element_type=F32) * xi_ref[h]
        o = jnp.dot(s.astype(BF16), vh, preferred_element_type=F32) + inter
        kz = (kh.astype(F32) * zt_ref[h]).astype(BF16)
        r_scr[h] = r_old * gc_ref[h] + lax.dot_general(
            kz, vh, (((0,), (0,)), ((), ())), preferred_element_type=F32)

        hc = slice(h * DV, (h + 1) * DV)
        on = _group_norm(o) * gn_ref[:, hc]
        r = sr_ref[:, hc].astype(F32) * on
        g, part = divmod(h * DV, og)
        a = a_groups[g][:, part:part + DV] * ps_ref[:, hc]
        mh = sa_ref[:, hc].astype(F32) * a + sg_ref[:, hc].astype(F32) * r
        m_ref[:, hc] = mh.astype(m_ref.dtype)

    @pl.when(c == pl.num_programs(1) - 1)
    def _():
        rout_ref[0] = r_scr[...]


def _mixer_prompt(u, rest, w_pool_b, pool_scale, gn_gain, *, batch, seq):
    m_rows, d_pool = u.shape
    d_v = N_HEADS * DV
    rb = RET_CHUNK
    assert seq % rb == 0 and m_rows == batch * seq
    nc = seq // rb
    dmask, xi_b, zeta_b, gc_b = _decay_tables(rb)
    d_qk = N_HEADS * DK
    row = lambda b, c: b * nc + c
    full = lambda shape: pl.BlockSpec(shape, lambda b, c: (0,) * len(shape))
    return pl.pallas_call(
        _mixer_prompt_kernel,
        grid=(batch, nc),
        in_specs=[
            pl.BlockSpec((rb, d_pool), lambda b, c: (row(b, c), 0)),
            pl.BlockSpec((rb, d_qk), lambda b, c: (row(b, c), 0)),
            pl.BlockSpec((rb, d_qk), lambda b, c: (row(b, c), 1)),
            pl.BlockSpec((rb, d_v), lambda b, c: (row(b, c), 1)),
            pl.BlockSpec((rb, d_v), lambda b, c: (row(b, c), 2)),
            pl.BlockSpec((rb, d_v), lambda b, c: (row(b, c), 3)),
            pl.BlockSpec((rb, d_v), lambda b, c: (row(b, c), 4)),
            full(w_pool_b.shape), full(pool_scale.shape), full(gn_gain.shape),
            full(dmask.shape), full(xi_b.shape), full(zeta_b.shape), full(gc_b.shape),
        ],
        out_specs=[
            pl.BlockSpec((rb, d_v), lambda b, c: (row(b, c), 0)),
            pl.BlockSpec((1, N_HEADS, DK, DV), lambda b, c: (b, 0, 0, 0)),
        ],
        out_shape=[
            jax.ShapeDtypeStruct((m_rows, d_v), BF16),
            jax.ShapeDtypeStruct((batch, N_HEADS, DK, DV), F32),
        ],
        scratch_shapes=[pltpu.VMEM((rb + POOL_MAX, d_pool), F32), pltpu.VMEM((N_HEADS, DK, DV), F32)],
        compiler_params=_params("arbitrary", "arbitrary"),
        name="mixer_prompt",
    )(u, rest, rest, rest, rest, rest, rest, w_pool_b, pool_scale, gn_gain, dmask, xi_b, zeta_b, gc_b)


def _mixer_sample_kernel(u_ref, sp_ref, q_ref, k_ref, v_ref, sr_ref, sa_ref, sg_ref, rin_ref,
                         wp_ref, ps_ref, gn_ref, dm_ref, xi_ref, zt_ref, gc_ref,
                         m_ref, rout_ref, ext_scr, *, pos0):
    bb, ln, d_pool = u_ref.shape
    hist = POOL_MAX - 1
    ext_scr[:, 1:1 + hist, :] = sp_ref[...]
    ext_scr[:, POOL_MAX:POOL_MAX + ln, :] = u_ref[...]

    pg = d_pool // len(POOL_WINDOWS)
    pos = pos0 + lax.broadcasted_iota(jnp.int32, (bb, ln, pg), 1)
    a_groups = []
    for g, w in enumerate(POOL_WINDOWS):
        cols = slice(g * pg, (g + 1) * pg)
        wsum = ext_scr[:, POOL_MAX:POOL_MAX + ln, cols]
        for lag in range(1, w):
            wsum = wsum + ext_scr[:, POOL_MAX - lag:POOL_MAX - lag + ln, cols]
        cnt = jnp.minimum(w, pos + 1).astype(F32)
        z = wsum / cnt - ext_scr[:, POOL_MAX:POOL_MAX + ln, cols]
        ag = jnp.dot(z.reshape(bb * ln, pg).astype(BF16), wp_ref[g], preferred_element_type=F32)
        a_groups.append(ag.reshape(bb, ln, ag.shape[-1]))
    og = a_groups[0].shape[-1]

    for h in range(N_HEADS):
        qh = q_ref[:, :, h * DK:(h + 1) * DK].astype(BF16)
        kf = k_ref[:, :, h * DK:(h + 1) * DK].astype(F32)
        kh = kf.astype(BF16)
        vh = v_ref[:, :, h * DV:(h + 1) * DV].astype(BF16)
        r_old = rin_ref[:, h]
        s = jnp.einsum("btd,bsd->bts", qh, kh, preferred_element_type=F32) * dm_ref[h]
        inter = jnp.einsum("btd,bdv->btv", qh, r_old.astype(BF16), preferred_element_type=F32) * xi_ref[h]
        o = jnp.einsum("bts,bsv->btv", s.astype(BF16), vh, preferred_element_type=F32) + inter
        kz_t = jnp.swapaxes(kf * zt_ref[h], 1, 2).astype(BF16)
        rout_ref[:, h] = r_old * gc_ref[h] + jnp.einsum(
            "bds,bsv->bdv", kz_t, vh, preferred_element_type=F32)

        hc = slice(h * DV, (h + 1) * DV)
        on = _group_norm(o) * gn_ref[:, hc]
        r = sr_ref[:, :, hc].astype(F32) * on
        g, part = divmod(h * DV, og)
        a = a_groups[g][:, :, part:part + DV] * ps_ref[:, hc]
        mh = sa_ref[:, :, hc].astype(F32) * a + sg_ref[:, :, hc].astype(F32) * r
        m_ref[:, :, hc] = mh.astype(m_ref.dtype)


def _mixer_sample(u, rest, state_pool, state_ret, w_pool_b, pool_scale, gn_gain, *, batch, seq, pos0, bb):
    d_pool = u.shape[-1]
    d_v, d_qk = N_HEADS * DV, N_HEADS * DK
    assert batch % bb == 0
    dmask, xi_b, zeta_b, gc_b = _decay_tables(seq)
    u3 = u.reshape(batch, seq, d_pool)
    rest3 = rest.reshape(batch, seq, rest.shape[-1])
    full = lambda shape: pl.BlockSpec(shape, lambda b: (0,) * len(shape))
    m3, r_new = pl.pallas_call(
        functools.partial(_mixer_sample_kernel, pos0=pos0),
        grid=(batch // bb,),
        in_specs=[
            pl.BlockSpec((bb, seq, d_pool), lambda b: (b, 0, 0)),
            pl.BlockSpec((bb, POOL_MAX - 1, d_pool), lambda b: (b, 0, 0)),
            pl.BlockSpec((bb, seq, d_qk), lambda b: (b, 0, 0)),
            pl.BlockSpec((bb, seq, d_qk), lambda b: (b, 0, 1)),
            pl.BlockSpec((bb, seq, d_v), lambda b: (b, 0, 1)),
            pl.BlockSpec((bb, seq, d_v), lambda b: (b, 0, 2)),
            pl.BlockSpec((bb, seq, d_v), lambda b: (b, 0, 3)),
            pl.BlockSpec((bb, seq, d_v), lambda b: (b, 0, 4)),
            pl.BlockSpec((bb, N_HEADS, DK, DV), lambda b: (b, 0, 0, 0)),
            full(w_pool_b.shape), full(pool_scale.shape), full(gn_gain.shape),
            full(dmask.shape), full(xi_b.shape), full(zeta_b.shape), full(gc_b.shape),
        ],
        out_specs=[
            pl.BlockSpec((bb, seq, d_v), lambda b: (b, 0, 0)),
            pl.BlockSpec((bb, N_HEADS, DK, DV), lambda b: (b, 0, 0, 0)),
        ],
        out_shape=[
            jax.ShapeDtypeStruct((batch, seq, d_v), F32),
            jax.ShapeDtypeStruct((batch, N_HEADS, DK, DV), F32),
        ],
        scratch_shapes=[pltpu.VMEM((bb, POOL_MAX + seq, d_pool), F32)],
        compiler_params=_params("arbitrary"),
        name="mixer_sample",
    )(u3, state_pool, rest3, rest3, rest3, rest3, rest3, rest3, state_ret,
      w_pool_b, pool_scale, gn_gain, dmask, xi_b, zeta_b, gc_b)
    return m3.reshape(batch * seq, d_v), r_new


def _out_proj_kernel(m_ref, w_ref, x_ref, gpost_ref, gpre_ref, x1_ref, h2_ref):
    proj = jnp.dot(m_ref[...].astype(BF16), w_ref[...], preferred_element_type=F32)
    x1 = x_ref[...] + _rms(proj, gpost_ref[...])
    x1_ref[...] = x1
    h2_ref[...] = _rms(x1, gpre_ref[...]).astype(BF16)


def _out_proj(m2, w_out_b, x2, g_post, g_pre):
    m, d = x2.shape
    tm = ROW_TILE
    assert m % tm == 0
    return pl.pallas_call(
        _out_proj_kernel,
        grid=(m // tm,),
        in_specs=[
            pl.BlockSpec((tm, m2.shape[1]), lambda i: (i, 0)),
            pl.BlockSpec(w_out_b.shape, lambda i: (0, 0)),
            pl.BlockSpec((tm, d), lambda i: (i, 0)),
            pl.BlockSpec((1, d), lambda i: (0, 0)),
            pl.BlockSpec((1, d), lambda i: (0, 0)),
        ],
        out_specs=[
            pl.BlockSpec((tm, d), lambda i: (i, 0)),
            pl.BlockSpec((tm, d), lambda i: (i, 0)),
        ],
        out_shape=[jax.ShapeDtypeStruct((m, d), F32), jax.ShapeDtypeStruct((m, d), BF16)],
        compiler_params=_params("arbitrary"),
        name="out_proj",
    )(m2, w_out_b, x2, g_post, g_pre)


def _gelu_tanh(x):
    return 0.5 * x * (1.0 + jnp.tanh(0.7978845608028654 * (x + 0.044715 * (x * x * x))))


def _conv_ffn_kernel(*refs, seq, blocks_per_seq):
    if seq >= ROW_TILE:
        (h2_ref, wv_ref, wg_ref, cwv_ref, cwg_ref, cbv_ref, cbg_ref, wd_ref, x1_ref, g_ref,
         y_ref, tail_ref, acc_scr, act_cur, act_prev, carry_scr) = refs
    else:
        (h2_ref, wv_ref, wg_ref, cwv_ref, cwg_ref, cbv_ref, cbg_ref, wd_ref, x1_ref, g_ref, stv_ref, stg_ref,
         y_ref, tail_ref, acc_scr, act_cur, act_prev) = refs
    i = pl.program_id(0)
    j = pl.program_id(1)
    nj = pl.num_programs(1) - 1
    tm = h2_ref.shape[0]
    hist = CONV_K - 1

    if seq >= ROW_TILE:
        @pl.when((i == 0) & (j == 0))
        def _():
            carry_scr[...] = jnp.zeros(carry_scr.shape, F32)

    def conv_long(part, up, cw_ref, cb_ref):
        prev = jnp.where(i % blocks_per_seq == 0, 0.0, carry_scr[j, part])
        row8 = lax.broadcasted_iota(jnp.int32, prev.shape, 0)
        out = up * cw_ref[hist:hist + 1, :] + cb_ref[...]
        for lag in range(1, CONV_K):
            rolled = pltpu.roll(up, lag, axis=0)
            top = jnp.where(row8 < lag, pltpu.roll(prev, lag, axis=0), rolled[0:8])
            shifted = jnp.concatenate([top, rolled[8:]], axis=0)
            out = out + shifted * cw_ref[hist - lag:hist - lag + 1, :]
        last = up[tm - 8:tm]
        carry_scr[j, part] = last
        tail_ref[0, part] = last
        return out

    def conv_short(part, up, cw_ref, cb_ref, st_ref):
        ns = tm // seq
        up3 = up.reshape(ns, seq, up.shape[1])
        t = lax.broadcasted_iota(jnp.int32, up3.shape, 1)
        out = up3 * cw_ref[hist:hist + 1, :] + cb_ref[...]
        for lag in range(1, CONV_K):
            shifted = pltpu.roll(up3, lag, axis=1)
            for r in range(lag):
                row = st_ref[:, hist - lag + r:hist - lag + r + 1, :]
                shifted = jnp.where(t == r, row, shifted)
            out = out + shifted * cw_ref[hist - lag:hist - lag + 1, :]
        tail_ref[:, part] = up3[:, seq - hist:, :]
        return out.reshape(tm, up.shape[1])

    def up_phase():
        h2 = h2_ref[...]
        upv = jnp.dot(h2, wv_ref[...], preferred_element_type=F32)
        upg = jnp.dot(h2, wg_ref[...], preferred_element_type=F32)
        if seq >= ROW_TILE:
            val = conv_long(0, upv, cwv_ref, cbv_ref)
            gate = conv_long(1, upg, cwg_ref, cbg_ref)
        else:
            val = conv_short(0, upv, cwv_ref, cbv_ref, stv_ref)
            gate = conv_short(1, upg, cwg_ref, cbg_ref, stg_ref)
        act_cur[...] = (_gelu_tanh(gate) * val).astype(BF16)

    @pl.when(j == 0)
    def _():
        acc_scr[...] = jnp.zeros(acc_scr.shape, F32)
        up_phase()

    @pl.when((j > 0) & (j < nj))
    def _():
        act_prev[...] = act_cur[...]
        up_phase()
        acc_scr[...] += jnp.dot(act_prev[...], wd_ref[...], preferred_element_type=F32)

    @pl.when(j == nj)
    def _():
        acc = acc_scr[...] + jnp.dot(act_cur[...], wd_ref[...], preferred_element_type=F32)
        y_ref[...] = x1_ref[...] + _rms(acc, g_ref[...])


def _conv_ffn(h2, x1, w_up_b, conv_w, conv_b, w_down_b, g_post, state_conv, *, seq):
    m, d = x1.shape
    d_ff = w_down_b.shape[0]
    tm, tf = ROW_TILE, FF_COL_TILE
    assert m % tm == 0 and d_ff % tf == 0
    nj = d_ff // tf
    hist = CONV_K - 1
    conv_b2 = conv_b.reshape(1, 2 * d_ff)
    long_seq = seq >= tm
    up_j = lambda j: jnp.minimum(j, nj - 1)
    down_j = lambda j: jnp.maximum(j - 1, 0)
    in_specs = [
        pl.BlockSpec((tm, d), lambda i, j: (i, 0)),
        pl.BlockSpec((d, tf), lambda i, j: (0, up_j(j))),
        pl.BlockSpec((d, tf), lambda i, j: (0, up_j(j) + nj)),
        pl.BlockSpec((CONV_K, tf), lambda i, j: (0, up_j(j))),
        pl.BlockSpec((CONV_K, tf), lambda i, j: (0, up_j(j) + nj)),
        pl.BlockSpec((1, tf), lambda i, j: (0, up_j(j))),
        pl.BlockSpec((1, tf), lambda i, j: (0, up_j(j) + nj)),
        pl.BlockSpec((tf, d), lambda i, j: (down_j(j), 0)),
        pl.BlockSpec((tm, d), lambda i, j: (i, 0)),
        pl.BlockSpec((1, d), lambda i, j: (0, 0)),
    ]
    args = [h2, w_up_b, w_up_b, conv_w, conv_w, conv_b2, conv_b2, w_down_b, x1, g_post]
    scratch = [pltpu.VMEM((tm, d), F32), pltpu.VMEM((tm, tf), BF16), pltpu.VMEM((tm, tf), BF16)]
    if long_seq:
        assert seq % tm == 0
        blocks_per_seq = seq // tm
        tail_shape = (m // tm, 2, 8, d_ff)
        tail_spec = pl.BlockSpec((1, 2, 8, tf), lambda i, j: (i, 0, 0, up_j(j)))
        scratch += [pltpu.VMEM((nj, 2, 8, tf), F32)]
    else:
        assert tm % seq == 0 and seq == 8 and seq >= hist
        blocks_per_seq = 0
        ns = tm // seq
        in_specs += [
            pl.BlockSpec((ns, hist, tf), lambda i, j: (i, 0, up_j(j))),
            pl.BlockSpec((ns, hist, tf), lambda i, j: (i, 0, up_j(j) + nj)),
        ]
        args += [state_conv, state_conv]
        tail_shape = (m // seq, 2, hist, d_ff)
        tail_spec = pl.BlockSpec((ns, 2, hist, tf), lambda i, j: (i, 0, 0, up_j(j)))
    y, tail = pl.pallas_call(
        functools.partial(_conv_ffn_kernel, seq=seq, blocks_per_seq=blocks_per_seq),
        grid=(m // tm, nj + 1),
        in_specs=in_specs,
        out_specs=[pl.BlockSpec((tm, d), lambda i, j: (i, 0)), tail_spec],
        out_shape=[jax.ShapeDtypeStruct((m, d), F32), jax.ShapeDtypeStruct(tail_shape, F32)],
        scratch_shapes=scratch,
        compiler_params=_params("arbitrary", "arbitrary"),
        name="conv_ffn",
    )(*args)
    if long_seq:
        tail = tail[blocks_per_seq - 1::blocks_per_seq, :, 8 - hist:, :]
    nseq = tail.shape[0]
    new_conv = jnp.swapaxes(tail, 1, 2).reshape(nseq, hist, 2 * d_ff)
    return y, new_conv


def _rotary_table(pos, rows):
    half = DK // 2
    theta = ROPE_BASE ** (-jnp.arange(half, dtype=F32) / half)
    ang = pos.astype(F32)[:, None] * theta[None, :]
    cos, sin = jnp.cos(ang), jnp.sin(ang)
    cs = jnp.concatenate([cos, cos, -sin, sin], axis=-1)
    reps = max(1, rows // cs.shape[0])
    return jnp.tile(cs, (reps, 1))


def kernel(x_prompt, x_sample, state_pool, state_ret, state_conv, g_pre_mix, w_in, w_pool, pool_scale, gn_gain,
           w_out, g_post_mix, g_pre_ffn, w_up, conv_w, conv_b, w_down, g_post_ffn):
    d = x_prompt.shape[-1]
    d_pool = state_pool.shape[-1]
    row = lambda v: v.reshape(1, -1)
    w_in_b, w_pool_b, w_out_b = w_in.astype(BF16), w_pool.astype(BF16), w_out.astype(BF16)
    w_up_b, w_down_b = w_up.astype(BF16), w_down.astype(BF16)

    def dense_tail(x2, m2, state_conv_path, seq):
        x1, h2 = _out_proj(m2, w_out_b, x2, row(g_post_mix), row(g_pre_ffn))
        return _conv_ffn(h2, x1, w_up_b, conv_w, conv_b, w_down_b, row(g_post_ffn), state_conv_path, seq=seq)

    bp, lp, _ = x_prompt.shape
    xp = x_prompt.reshape(bp * lp, d)
    cs_p = _rotary_table(jnp.arange(lp, dtype=jnp.int32), IN_ROW_TILE)
    u_p, rest_p = _in_proj(xp, row(g_pre_mix), cs_p, w_in_b, d_pool=d_pool, rest_dtype=BF16)
    m_p, ret_p = _mixer_prompt(u_p, rest_p, w_pool_b, row(pool_scale), row(gn_gain), batch=bp, seq=lp)
    y_p, conv_p = dense_tail(xp, m_p, None, lp)
    pool_p = u_p.reshape(bp, lp, d_pool)[:, lp - (POOL_MAX - 1):]

    bs, ls, _ = x_sample.shape
    xs = x_sample.reshape(bs * ls, d)
    cs_s = _rotary_table(PAST_LEN + jnp.arange(ls, dtype=jnp.int32), IN_ROW_TILE)
    u_s, rest_s = _in_proj(xs, row(g_pre_mix), cs_s, w_in_b, d_pool=d_pool, rest_dtype=F32)
    m_s, ret_s = _mixer_sample(u_s, rest_s, state_pool, state_ret, w_pool_b, row(pool_scale), row(gn_gain),
                               batch=bs, seq=ls, pos0=PAST_LEN, bb=4)
    y_s, conv_s = dense_tail(xs, m_s, state_conv, ls)
    pool_s = jnp.concatenate([state_pool, u_s.reshape(bs, ls, d_pool)], axis=1)[:, -(POOL_MAX - 1):]

    return (y_p.reshape(bp, lp, d), y_s.reshape(bs, ls, d), pool_p, ret_p.astype(x_prompt.dtype), conv_p,
            pool_s, ret_s.astype(x_sample.dtype), conv_s)
```

```python
import functools

import jax
import jax.numpy as jnp
from jax import lax
from jax.experimental import pallas as pl
from jax.experimental.pallas import tpu as pltpu

F32 = jnp.float32
BF16 = jnp.bfloat16

EPS = 1e-6
N_HEADS = 8
DK = 128
DV = 256
POOL_WINDOWS = (2, 4, 8, 16)
POOL_MAX = 16
CONV_K = 3
ROPE_BASE = 10000.0
PAST_LEN = 16384
RET_CHUNK = 128

VMEM_LIMIT_BYTES = 56 * 1024 * 1024
OUT_ROW_TILE = 512
IN_ROW_TILE = 1024
FF_ROW_TILE = 1024
IN_COL_TILE = 1024
FF_COL_TILE = 512


def _params(*sem):
    return pltpu.CompilerParams(dimension_semantics=sem, vmem_limit_bytes=VMEM_LIMIT_BYTES)


def _rms(x, g):
    return x * lax.rsqrt(jnp.mean(x * x, axis=-1, keepdims=True) + EPS) * g


def _in_proj_kernel(x_ref, g_ref, cs_ref, w_ref, u_ref, p_ref, h_scr, *, kinds):
    j = pl.program_id(1)

    @pl.when(j == 0)
    def _():
        h_scr[...] = _rms(x_ref[...], g_ref[...]).astype(BF16)

    def proj():
        return jnp.dot(h_scr[...], w_ref[...], preferred_element_type=F32)

    def rotary(acc, scale):
        cos = cs_ref[:, :DK]
        sin = cs_ref[:, DK:]
        for h in range(acc.shape[1] // DK):
            xh = acc[:, h * DK:(h + 1) * DK]
            rot = xh * cos + pltpu.roll(xh, DK // 2, axis=1) * sin
            if scale != 1.0:
                rot = rot * scale
            p_ref[:, h * DK:(h + 1) * DK] = rot.astype(p_ref.dtype)

    def tiles_of(kind):
        return [t for t, k in enumerate(kinds) if k == kind]

    def cond_of(kind):
        ts = tiles_of(kind)
        return (j >= ts[0]) & (j <= ts[-1])

    @pl.when(cond_of("u"))
    def _():
        u_ref[...] = proj()

    @pl.when(cond_of("q"))
    def _():
        rotary(proj(), 1.0)

    @pl.when(cond_of("k"))
    def _():
        rotary(proj(), DK ** -0.5)

    @pl.when(cond_of("v"))
    def _():
        p_ref[...] = proj().astype(p_ref.dtype)

    @pl.when(cond_of("silu"))
    def _():
        acc = proj()
        p_ref[...] = (acc * jax.nn.sigmoid(acc)).astype(p_ref.dtype)

    @pl.when(cond_of("sig"))
    def _():
        p_ref[...] = jax.nn.sigmoid(proj()).astype(p_ref.dtype)


def _in_proj(x2, g, cs, w_in_b, *, d_pool, rest_dtype):
    m, d = x2.shape
    n_in = w_in_b.shape[1]
    tm, tn = IN_ROW_TILE, IN_COL_TILE
    assert m % tm == 0 and n_in % tn == 0 and d_pool == tn and cs.shape[0] % tm == 0
    d_qk, d_v = N_HEADS * DK, N_HEADS * DV
    kinds = (["u"] * (d_pool // tn) + ["q"] * (d_qk // tn) + ["k"] * (d_qk // tn) + ["v"] * (d_v // tn)
             + ["silu"] * (d_v // tn) + ["sig"] * (2 * d // tn))
    assert len(kinds) == n_in // tn
    cs_blocks = cs.shape[0] // tm
    return pl.pallas_call(
        functools.partial(_in_proj_kernel, kinds=tuple(kinds)),
        grid=(m // tm, n_in // tn),
        in_specs=[
            pl.BlockSpec((tm, d), lambda i, j: (i, 0)),
            pl.BlockSpec((1, d), lambda i, j: (0, 0)),
            pl.BlockSpec((tm, 2 * DK), lambda i, j: (i % cs_blocks, 0)),
            pl.BlockSpec((d, tn), lambda i, j: (0, j)),
        ],
        out_specs=[
            pl.BlockSpec((tm, d_pool), lambda i, j: (i, 0)),
            pl.BlockSpec((tm, tn), lambda i, j: (i, jnp.maximum(j - d_pool // tn, 0))),
        ],
        out_shape=[
            jax.ShapeDtypeStruct((m, d_pool), F32),
            jax.ShapeDtypeStruct((m, n_in - d_pool), rest_dtype),
        ],
        scratch_shapes=[pltpu.VMEM((tm, d), BF16)],
        compiler_params=_params("arbitrary", "arbitrary"),
        name="in_proj",
    )(x2, g, cs, w_in_b)


def _group_norm(o):
    mu = jnp.mean(o, axis=-1, keepdims=True)
    d = o - mu
    var = jnp.mean(d * d, axis=-1, keepdims=True)
    return d * lax.rsqrt(var + EPS)


def _decay_tables(c):
    log_g = jnp.log(1.0 - 2.0 ** (-5.0 - jnp.arange(N_HEADS, dtype=F32)))
    idx = jnp.arange(c, dtype=F32)
    rel = idx[:, None] - idx[None, :]
    dmask = jnp.where(rel >= 0, jnp.exp(jnp.maximum(rel, 0.0)[None] * log_g[:, None, None]), 0.0)
    xi = jnp.exp((idx + 1.0)[None, :] * log_g[:, None])
    zeta = jnp.exp((c - 1.0 - idx)[None, :] * log_g[:, None])
    g_c = jnp.exp(c * log_g)
    xi_b = jnp.broadcast_to(xi[:, :, None], (N_HEADS, c, DV))
    zeta_b = jnp.broadcast_to(zeta[:, :, None], (N_HEADS, c, DK))
    gc_b = jnp.broadcast_to(g_c[:, None, None], (N_HEADS, 1, DV))
    return dmask, xi_b, zeta_b, gc_b


def _mixer_prompt_kernel(u_ref, q_ref, k_ref, v_ref, sr_ref, sa_ref, sg_ref,
                         wp_ref, ps_ref, gn_ref, dm_ref, xi_ref, zt_ref, gc_ref,
                         m_ref, rout_ref, ext_scr, r_scr):
    c = pl.program_id(1)
    rb = u_ref.shape[0]
    halo = POOL_MAX

    @pl.when(c == 0)
    def _():
        ext_scr[0:halo, :] = jnp.zeros((halo, ext_scr.shape[1]), F32)
        r_scr[...] = jnp.zeros(r_scr.shape, F32)

    ext_scr[halo:halo + rb, :] = u_ref[...]

    pg = u_ref.shape[1] // len(POOL_WINDOWS)
    pos = c * rb + lax.broadcasted_iota(jnp.int32, (rb, pg), 0)
    a_groups = []
    for g, w in enumerate(POOL_WINDOWS):
        cols = slice(g * pg, (g + 1) * pg)
        wsum = ext_scr[halo:halo + rb, cols]
        for lag in range(1, w):
            wsum = wsum + ext_scr[halo - lag:halo - lag + rb, cols]
        cnt = jnp.minimum(w, pos + 1).astype(F32)
        z = wsum / cnt - ext_scr[halo:halo + rb, cols]
        a_groups.append(jnp.dot(z.astype(BF16), wp_ref[g], preferred_element_type=F32))
    og = a_groups[0].shape[1]

    ext_scr[0:halo, :] = ext_scr[rb:rb + halo, :]

    for h in range(N_HEADS):
        qh = q_ref[:, h * DK:(h + 1) * DK]
        kh = k_ref[:, h * DK:(h + 1) * DK]
        vh = v_ref[:, h * DV:(h + 1) * DV]
        r_old = r_scr[h]
        s = lax.dot_general(qh, kh, (((1,), (1,)), ((), ())), preferred_element_type=F32) * dm_ref[h]
        inter = jnp.dot(qh, r_old.astype(BF16), preferred_element_type=F32) * xi_ref[h]
        o = jnp.dot(s.astype(BF16), vh, preferred_element_type=F32) + inter
        kz = (kh.astype(F32) * zt_ref[h]).astype(BF16)
        r_scr[h] = r_old * gc_ref[h] + lax.dot_general(
            kz, vh, (((0,), (0,)), ((), ())), preferred_element_type=F32)

        hc = slice(h * DV, (h + 1) * DV)
        on = _group_norm(o) * gn_ref[:, hc]
        r = sr_ref[:, hc].astype(F32) * on
        g, part = divmod(h * DV, og)
        a = a_groups[g][:, part:part + DV] * ps_ref[:, hc]
        mh = sa_ref[:, hc].astype(F32) * a + sg_ref[:, hc].astype(F32) * r
        m_ref[:, hc] = mh.astype(m_ref.dtype)

    @pl.when(c == pl.num_programs(1) - 1)
    def _():
        rout_ref[0] = r_scr[...]


def _mixer_prompt(u, rest, w_pool_b, pool_scale, gn_gain, *, batch, seq):
    m_rows, d_pool = u.shape
    d_v = N_HEADS * DV
    rb = RET_CHUNK
    assert seq % rb == 0 and m_rows == batch * seq
    nc = seq // rb
    dmask, xi_b, zeta_b, gc_b = _decay_tables(rb)
    d_qk = N_HEADS * DK
    row = lambda b, c: b * nc + c
    full = lambda shape: pl.BlockSpec(shape, lambda b, c: (0,) * len(shape))
    return pl.pallas_call(
        _mixer_prompt_kernel,
        grid=(batch, nc),
        in_specs=[
            pl.BlockSpec((rb, d_pool), lambda b, c: (row(b, c), 0)),
            pl.BlockSpec((rb, d_qk), lambda b, c: (row(b, c), 0)),
            pl.BlockSpec((rb, d_qk), lambda b, c: (row(b, c), 1)),
            pl.BlockSpec((rb, d_v), lambda b, c: (row(b, c), 1)),
            pl.BlockSpec((rb, d_v), lambda b, c: (row(b, c), 2)),
            pl.BlockSpec((rb, d_v), lambda b, c: (row(b, c), 3)),
            pl.BlockSpec((rb, d_v), lambda b, c: (row(b, c), 4)),
            full(w_pool_b.shape), full(pool_scale.shape), full(gn_gain.shape),
            full(dmask.shape), full(xi_b.shape), full(zeta_b.shape), full(gc_b.shape),
        ],
        out_specs=[
            pl.BlockSpec((rb, d_v), lambda b, c: (row(b, c), 0)),
            pl.BlockSpec((1, N_HEADS, DK, DV), lambda b, c: (b, 0, 0, 0)),
        ],
        out_shape=[
            jax.ShapeDtypeStruct((m_rows, d_v), BF16),
            jax.ShapeDtypeStruct((batch, N_HEADS, DK, DV), F32),
        ],
        scratch_shapes=[pltpu.VMEM((rb + POOL_MAX, d_pool), F32), pltpu.VMEM((N_HEADS, DK, DV), F32)],
        compiler_params=_params("arbitrary", "arbitrary"),
        name="mixer_prompt",
    )(u, rest, rest, rest, rest, rest, rest, w_pool_b, pool_scale, gn_gain, dmask, xi_b, zeta_b, gc_b)


def _mixer_sample_kernel(u_ref, sp_ref, q_ref, k_ref, v_ref, sr_ref, sa_ref, sg_ref, rin_ref,
                         wp_ref, ps_ref, gn_ref, dm_ref, xi_ref, zt_ref, gc_ref,
                         m_ref, rout_ref, ext_scr, *, pos0):
    bb, ln, d_pool = u_ref.shape
    hist = POOL_MAX - 1
    ext_scr[:, 1:1 + hist, :] = sp_ref[...]
    ext_scr[:, POOL_MAX:POOL_MAX + ln, :] = u_ref[...]

    pg = d_pool // len(POOL_WINDOWS)
    pos = pos0 + lax.broadcasted_iota(jnp.int32, (bb, ln, pg), 1)
    a_groups = []
    for g, w in enumerate(POOL_WINDOWS):
        cols = slice(g * pg, (g + 1) * pg)
        wsum = ext_scr[:, POOL_MAX:POOL_MAX + ln, cols]
        for lag in range(1, w):
            wsum = wsum + ext_scr[:, POOL_MAX - lag:POOL_MAX - lag + ln, cols]
        cnt = jnp.minimum(w, pos + 1).astype(F32)
        z = wsum / cnt - ext_scr[:, POOL_MAX:POOL_MAX + ln, cols]
        ag = jnp.dot(z.reshape(bb * ln, pg).astype(BF16), wp_ref[g], preferred_element_type=F32)
        a_groups.append(ag.reshape(bb, ln, ag.shape[-1]))
    og = a_groups[0].shape[-1]

    for h in range(N_HEADS):
        qh = q_ref[:, :, h * DK:(h + 1) * DK].astype(BF16)
        kf = k_ref[:, :, h * DK:(h + 1) * DK].astype(F32)
        kh = kf.astype(BF16)
        vh = v_ref[:, :, h * DV:(h + 1) * DV].astype(BF16)
        r_old = rin_ref[:, h]
        s = jnp.einsum("btd,bsd->bts", qh, kh, preferred_element_type=F32) * dm_ref[h]
        inter = jnp.einsum("btd,bdv->btv", qh, r_old.astype(BF16), preferred_element_type=F32) * xi_ref[h]
        o = jnp.einsum("bts,bsv->btv", s.astype(BF16), vh, preferred_element_type=F32) + inter
        kz_t = jnp.swapaxes(kf * zt_ref[h], 1, 2).astype(BF16)
        rout_ref[:, h] = r_old * gc_ref[h] + jnp.einsum(
            "bds,bsv->bdv", kz_t, vh, preferred_element_type=F32)

        hc = slice(h * DV, (h + 1) * DV)
        on = _group_norm(o) * gn_ref[:, hc]
        r = sr_ref[:, :, hc].astype(F32) * on
        g, part = divmod(h * DV, og)
        a = a_groups[g][:, :, part:part + DV] * ps_ref[:, hc]
        mh = sa_ref[:, :, hc].astype(F32) * a + sg_ref[:, :, hc].astype(F32) * r
        m_ref[:, :, hc] = mh.astype(m_ref.dtype)


def _mixer_sample(u, rest, state_pool, state_ret, w_pool_b, pool_scale, gn_gain, *, batch, seq, pos0, bb):
    d_pool = u.shape[-1]
    d_v, d_qk = N_HEADS * DV, N_HEADS * DK
    assert batch % bb == 0
    dmask, xi_b, zeta_b, gc_b = _decay_tables(seq)
    u3 = u.reshape(batch, seq, d_pool)
    rest3 = rest.reshape(batch, seq, rest.shape[-1])
    full = lambda shape: pl.BlockSpec(shape, lambda b: (0,) * len(shape))
    m3, r_new = pl.pallas_call(
        functools.partial(_mixer_sample_kernel, pos0=pos0),
        grid=(batch // bb,),
        in_specs=[
            pl.BlockSpec((bb, seq, d_pool), lambda b: (b, 0, 0)),
            pl.BlockSpec((bb, POOL_MAX - 1, d_pool), lambda b: (b, 0, 0)),
            pl.BlockSpec((bb, seq, d_qk), lambda b: (b, 0, 0)),
            pl.BlockSpec((bb, seq, d_qk), lambda b: (b, 0, 1)),
            pl.BlockSpec((bb, seq, d_v), lambda b: (b, 0, 1)),
            pl.BlockSpec((bb, seq, d_v), lambda b: (b, 0, 2)),
            pl.BlockSpec((bb, seq, d_v), lambda b: (b, 0, 3)),
            pl.BlockSpec((bb, seq, d_v), lambda b: (b, 0, 4)),
            pl.BlockSpec((bb, N_HEADS, DK, DV), lambda b: (b, 0, 0, 0)),
            full(w_pool_b.shape), full(pool_scale.shape), full(gn_gain.shape),
            full(dmask.shape), full(xi_b.shape), full(zeta_b.shape), full(gc_b.shape),
        ],
        out_specs=[
            pl.BlockSpec((bb, seq, d_v), lambda b: (b, 0, 0)),
            pl.BlockSpec((bb, N_HEADS, DK, DV), lambda b: (b, 0, 0, 0)),
        ],
        out_shape=[
            jax.ShapeDtypeStruct((batch, seq, d_v), F32),
            jax.ShapeDtypeStruct((batch, N_HEADS, DK, DV), F32),
        ],
        scratch_shapes=[pltpu.VMEM((bb, POOL_MAX + seq, d_pool), F32)],
        compiler_params=_params("arbitrary"),
        name="mixer_sample",
    )(u3, state_pool, rest3, rest3, rest3, rest3, rest3, rest3, state_ret,
      w_pool_b, pool_scale, gn_gain, dmask, xi_b, zeta_b, gc_b)
    return m3.reshape(batch * seq, d_v), r_new


def _out_proj_kernel(m_ref, w_ref, x_ref, gpost_ref, gpre_ref, x1_ref, h2_ref):
    proj = jnp.dot(m_ref[...].astype(BF16), w_ref[...], preferred_element_type=F32)
    x1 = x_ref[...] + _rms(proj, gpost_ref[...])
    x1_ref[...] = x1
    h2_ref[...] = _rms(x1, gpre_ref[...]).astype(BF16)


def _out_proj(m2, w_out_b, x2, g_post, g_pre):
    m, d = x2.shape
    tm = OUT_ROW_TILE
    assert m % tm == 0
    return pl.pallas_call(
        _out_proj_kernel,
        grid=(m // tm,),
        in_specs=[
            pl.BlockSpec((tm, m2.shape[1]), lambda i: (i, 0)),
            pl.BlockSpec(w_out_b.shape, lambda i: (0, 0)),
            pl.BlockSpec((tm, d), lambda i: (i, 0)),
            pl.BlockSpec((1, d), lambda i: (0, 0)),
            pl.BlockSpec((1, d), lambda i: (0, 0)),
        ],
        out_specs=[
            pl.BlockSpec((tm, d), lambda i: (i, 0)),
            pl.BlockSpec((tm, d), lambda i: (i, 0)),
        ],
        out_shape=[jax.ShapeDtypeStruct((m, d), F32), jax.ShapeDtypeStruct((m, d), BF16)],
        compiler_params=_params("arbitrary"),
        name="out_proj",
    )(m2, w_out_b, x2, g_post, g_pre)


def _gelu_tanh(x):
    return 0.5 * x * (1.0 + jnp.tanh(0.7978845608028654 * (x + 0.044715 * (x * x * x))))


def _conv_ffn_kernel(*refs, seq, blocks_per_seq):
    if seq >= FF_ROW_TILE:
        (h2_ref, wv_ref, wg_ref, cwv_ref, cwg_ref, cbv_ref, cbg_ref, wd_ref, x1_ref, g_ref,
         y_ref, tail_ref, act_cur, act_prev, carry_scr) = refs
    else:
        (h2_ref, wv_ref, wg_ref, cwv_ref, cwg_ref, cbv_ref, cbg_ref, wd_ref, x1_ref, g_ref, stv_ref, stg_ref,
         y_ref, tail_ref, act_cur, act_prev) = refs
    i = pl.program_id(0)
    j = pl.program_id(1)
    nj = pl.num_programs(1) - 1
    tm = h2_ref.shape[0]
    hist = CONV_K - 1

    if seq >= FF_ROW_TILE:
        @pl.when((i == 0) & (j == 0))
        def _():
            carry_scr[...] = jnp.zeros(carry_scr.shape, F32)

    def conv_long(part, up, cw_ref, cb_ref):
        prev = jnp.where(i % blocks_per_seq == 0, 0.0, carry_scr[j, part])
        row8 = lax.broadcasted_iota(jnp.int32, prev.shape, 0)
        out = up * cw_ref[hist:hist + 1, :] + cb_ref[...]
        for lag in range(1, CONV_K):
            rolled = pltpu.roll(up, lag, axis=0)
            top = jnp.where(row8 < lag, pltpu.roll(prev, lag, axis=0), rolled[0:8])
            shifted = jnp.concatenate([top, rolled[8:]], axis=0)
            out = out + shifted * cw_ref[hist - lag:hist - lag + 1, :]
        last = up[tm - 8:tm]
        carry_scr[j, part] = last
        tail_ref[0, part] = last
        return out

    def conv_short(part, up, cw_ref, cb_ref, st_ref):
        ns = tm // seq
        up3 = up.reshape(ns, seq, up.shape[1])
        t = lax.broadcasted_iota(jnp.int32, up3.shape, 1)
        out = up3 * cw_ref[hist:hist + 1, :] + cb_ref[...]
        for lag in range(1, CONV_K):
            shifted = pltpu.roll(up3, lag, axis=1)
            for r in range(lag):
                row = st_ref[:, hist - lag + r:hist - lag + r + 1, :]
                shifted = jnp.where(t == r, row, shifted)
            out = out + shifted * cw_ref[hist - lag:hist - lag + 1, :]
        tail_ref[:, part] = up3[:, seq - hist:, :]
        return out.reshape(tm, up.shape[1])

    def up_phase():
        h2 = h2_ref[...]
        upv = jnp.dot(h2, wv_ref[...], preferred_element_type=F32)
        upg = jnp.dot(h2, wg_ref[...], preferred_element_type=F32)
        if seq >= FF_ROW_TILE:
            val = conv_long(0, upv, cwv_ref, cbv_ref)
            gate = conv_long(1, upg, cwg_ref, cbg_ref)
        else:
            val = conv_short(0, upv, cwv_ref, cbv_ref, stv_ref)
            gate = conv_short(1, upg, cwg_ref, cbg_ref, stg_ref)
        act_cur[...] = (_gelu_tanh(gate) * val).astype(BF16)

    @pl.when(j == 0)
    def _():
        y_ref[...] = jnp.zeros(y_ref.shape, F32)
        up_phase()

    @pl.when((j > 0) & (j < nj))
    def _():
        act_prev[...] = act_cur[...]
        up_phase()
        y_ref[...] += jnp.dot(act_prev[...], wd_ref[...], preferred_element_type=F32)

    @pl.when(j == nj)
    def _():
        acc = y_ref[...] + jnp.dot(act_cur[...], wd_ref[...], preferred_element_type=F32)
        y_ref[...] = x1_ref[...] + _rms(acc, g_ref[...])


def _conv_ffn(h2, x1, w_up_b, conv_w, conv_b, w_down_b, g_post, state_conv, *, seq):
    m, d = x1.shape
    d_ff = w_down_b.shape[0]
    tm, tf = FF_ROW_TILE, FF_COL_TILE
    assert m % tm == 0 and d_ff % tf == 0
    nj = d_ff // tf
    hist = CONV_K - 1
    conv_b2 = conv_b.reshape(1, 2 * d_ff)
    long_seq = seq >= tm
    up_j = lambda j: jnp.minimum(j, nj - 1)
    down_j = lambda j: jnp.maximum(j - 1, 0)
    in_specs = [
        pl.BlockSpec((tm, d), lambda i, j: (i, 0)),
        pl.BlockSpec((d, tf), lambda i, j: (0, up_j(j))),
        pl.BlockSpec((d, tf), lambda i, j: (0, up_j(j) + nj)),
        pl.BlockSpec((CONV_K, tf), lambda i, j: (0, up_j(j))),
        pl.BlockSpec((CONV_K, tf), lambda i, j: (0, up_j(j) + nj)),
        pl.BlockSpec((1, tf), lambda i, j: (0, up_j(j))),
        pl.BlockSpec((1, tf), lambda i, j: (0, up_j(j) + nj)),
        pl.BlockSpec((tf, d), lambda i, j: (down_j(j), 0)),
        pl.BlockSpec((tm, d), lambda i, j: (i, 0), pipeline_mode=pl.Buffered(1)),
        pl.BlockSpec((1, d), lambda i, j: (0, 0)),
    ]
    args = [h2, w_up_b, w_up_b, conv_w, conv_w, conv_b2, conv_b2, w_down_b, x1, g_post]
    scratch = [pltpu.VMEM((tm, tf), BF16), pltpu.VMEM((tm, tf), BF16)]
    if long_seq:
        assert seq % tm == 0
        blocks_per_seq = seq // tm
        tail_shape = (m // tm, 2, 8, d_ff)
        tail_spec = pl.BlockSpec((1, 2, 8, tf), lambda i, j: (i, 0, 0, up_j(j)))
        scratch += [pltpu.VMEM((nj, 2, 8, tf), F32)]
    else:
        assert tm % seq == 0 and seq == 8 and seq >= hist
        blocks_per_seq = 0
        ns = tm // seq
        in_specs += [
            pl.BlockSpec((ns, hist, tf), lambda i, j: (i, 0, up_j(j))),
            pl.BlockSpec((ns, hist, tf), lambda i, j: (i, 0, up_j(j) + nj)),
        ]
        args += [state_conv, state_conv]
        tail_shape = (m // seq, 2, hist, d_ff)
        tail_spec = pl.BlockSpec((ns, 2, hist, tf), lambda i, j: (i, 0, 0, up_j(j)))
    y, tail = pl.pallas_call(
        functools.partial(_conv_ffn_kernel, seq=seq, blocks_per_seq=blocks_per_seq),
        grid=(m // tm, nj + 1),
        in_specs=in_specs,
        out_specs=[pl.BlockSpec((tm, d), lambda i, j: (i, 0)), tail_spec],
        out_shape=[jax.ShapeDtypeStruct((m, d), F32), jax.ShapeDtypeStruct(tail_shape, F32)],
        scratch_shapes=scratch,
        compiler_params=_params("arbitrary", "arbitrary"),
        name="conv_ffn",
    )(*args)
    if long_seq:
        tail = tail[blocks_per_seq - 1::blocks_per_seq, :, 8 - hist:, :]
    nseq = tail.shape[0]
    new_conv = jnp.swapaxes(tail, 1, 2).reshape(nseq, hist, 2 * d_ff)
    return y, new_conv


def _rotary_table(pos, rows):
    half = DK // 2
    theta = ROPE_BASE ** (-jnp.arange(half, dtype=F32) / half)
    ang = pos.astype(F32)[:, None] * theta[None, :]
    cos, sin = jnp.cos(ang), jnp.sin(ang)
    cs = jnp.concatenate([cos, cos, -sin, sin], axis=-1)
    reps = max(1, rows // cs.shape[0])
    return jnp.tile(cs, (reps, 1))


def kernel(x_prompt, x_sample, state_pool, state_ret, state_conv, g_pre_mix, w_in, w_pool, pool_scale, gn_gain,
           w_out, g_post_mix, g_pre_ffn, w_up, conv_w, conv_b, w_down, g_post_ffn):
    d = x_prompt.shape[-1]
    d_pool = state_pool.shape[-1]
    row = lambda v: v.reshape(1, -1)
    w_in_b, w_pool_b, w_out_b = w_in.astype(BF16), w_pool.astype(BF16), w_out.astype(BF16)
    w_up_b, w_down_b = w_up.astype(BF16), w_down.astype(BF16)

    def dense_tail(x2, m2, state_conv_path, seq):
        x1, h2 = _out_proj(m2, w_out_b, x2, row(g_post_mix), row(g_pre_ffn))
        return _conv_ffn(h2, x1, w_up_b, conv_w, conv_b, w_down_b, row(g_post_ffn), state_conv_path, seq=seq)

    bp, lp, _ = x_prompt.shape
    xp = x_prompt.reshape(bp * lp, d)
    cs_p = _rotary_table(jnp.arange(lp, dtype=jnp.int32), IN_ROW_TILE)
    u_p, rest_p = _in_proj(xp, row(g_pre_mix), cs_p, w_in_b, d_pool=d_pool, rest_dtype=BF16)
    m_p, ret_p = _mixer_prompt(u_p, rest_p, w_pool_b, row(pool_scale), row(gn_gain), batch=bp, seq=lp)
    y_p, conv_p = dense_tail(xp, m_p, None, lp)
    pool_p = u_p.reshape(bp, lp, d_pool)[:, lp - (POOL_MAX - 1):]

    bs, ls, _ = x_sample.shape
    xs = x_sample.reshape(bs * ls, d)
    cs_s = _rotary_table(PAST_LEN + jnp.arange(ls, dtype=jnp.int32), IN_ROW_TILE)
    u_s, rest_s = _in_proj(xs, row(g_pre_mix), cs_s, w_in_b, d_pool=d_pool, rest_dtype=F32)
    m_s, ret_s = _mixer_sample(u_s, rest_s, state_pool, state_ret, w_pool_b, row(pool_scale), row(gn_gain),
                               batch=bs, seq=ls, pos0=PAST_LEN, bb=4)
    y_s, conv_s = dense_tail(xs, m_s, state_conv, ls)
    pool_s = jnp.concatenate([state_pool, u_s.reshape(bs, ls, d_pool)], axis=1)[:, -(POOL_MAX - 1):]

    return (y_p.reshape(bp, lp, d), y_s.reshape(bs, ls, d), pool_p, ret_p.astype(x_prompt.dtype), conv_p,
            pool_s, ret_s.astype(x_sample.dtype), conv_s)
```

```python
import functools

import jax
import jax.numpy as jnp
from jax import lax
from jax.experimental import pallas as pl
from jax.experimental.pallas import tpu as pltpu

F32 = jnp.float32
BF16 = jnp.bfloat16

EPS = 1e-6
N_HEADS = 8
DK = 128
DV = 256
POOL_WINDOWS = (2, 4, 8, 16)
POOL_MAX = 16
CONV_K = 3
ROPE_BASE = 10000.0
PAST_LEN = 16384
RET_CHUNK = 128

VMEM_LIMIT_BYTES = 60 * 1024 * 1024
OUT_ROW_TILE = 512
IN_ROW_TILE = 1024
FF_ROW_TILE = 1024
IN_COL_TILE = 1024
FF_COL_TILE = 512
FF_EW_ROWS = 64
FF_EW_COLS = 256


def _params(*sem, flags=None):
    return pltpu.CompilerParams(dimension_semantics=sem, vmem_limit_bytes=VMEM_LIMIT_BYTES, flags=flags)


def _rms(x, g):
    return x * lax.rsqrt(jnp.mean(x * x, axis=-1, keepdims=True) + EPS) * g


def _in_proj_kernel(x_ref, g_ref, cs_ref, w_ref, u_ref, p_ref, h_scr, *, kinds):
    j = pl.program_id(1)

    @pl.when(j == 0)
    def _():
        h_scr[...] = _rms(x_ref[...], g_ref[...]).astype(BF16)

    def proj():
        return jnp.dot(h_scr[...], w_ref[...], preferred_element_type=F32)

    def rotary(acc, scale):
        cos = cs_ref[:, :DK]
        sin = cs_ref[:, DK:]
        for h in range(acc.shape[1] // DK):
            xh = acc[:, h * DK:(h + 1) * DK]
            rot = xh * cos + pltpu.roll(xh, DK // 2, axis=1) * sin
            if scale != 1.0:
                rot = rot * scale
            p_ref[:, h * DK:(h + 1) * DK] = rot.astype(p_ref.dtype)

    def tiles_of(kind):
        return [t for t, k in enumerate(kinds) if k == kind]

    def cond_of(kind):
        ts = tiles_of(kind)
        return (j >= ts[0]) & (j <= ts[-1])

    @pl.when(cond_of("u"))
    def _():
        u_ref[...] = proj()

    @pl.when(cond_of("q"))
    def _():
        rotary(proj(), 1.0)

    @pl.when(cond_of("k"))
    def _():
        rotary(proj(), DK ** -0.5)

    @pl.when(cond_of("v"))
    def _():
        p_ref[...] = proj().astype(p_ref.dtype)

    @pl.when(cond_of("silu"))
    def _():
        acc = proj()
        p_ref[...] = (acc * jax.nn.sigmoid(acc)).astype(p_ref.dtype)

    @pl.when(cond_of("sig"))
    def _():
        p_ref[...] = jax.nn.sigmoid(proj()).astype(p_ref.dtype)


def _in_proj(x2, g, cs, w_in_t, *, d_pool, rest_dtype):
    m, d = x2.shape
    tm = IN_ROW_TILE
    n_tiles, _, tn = w_in_t.shape
    n_in = n_tiles * tn
    assert m % tm == 0 and d_pool == tn and cs.shape[0] % tm == 0
    d_qk, d_v = N_HEADS * DK, N_HEADS * DV
    kinds = (["u"] * (d_pool // tn) + ["q"] * (d_qk // tn) + ["k"] * (d_qk // tn) + ["v"] * (d_v // tn)
             + ["silu"] * (d_v // tn) + ["sig"] * (2 * d // tn))
    assert len(kinds) == n_in // tn
    cs_blocks = cs.shape[0] // tm
    return pl.pallas_call(
        functools.partial(_in_proj_kernel, kinds=tuple(kinds)),
        grid=(m // tm, n_in // tn),
        in_specs=[
            pl.BlockSpec((tm, d), lambda i, j: (i, 0)),
            pl.BlockSpec((1, d), lambda i, j: (0, 0)),
            pl.BlockSpec((tm, 2 * DK), lambda i, j: (i % cs_blocks, 0)),
            pl.BlockSpec((pl.Squeezed(), d, tn), lambda i, j: (j, 0, 0)),
        ],
        out_specs=[
            pl.BlockSpec((tm, d_pool), lambda i, j: (i, 0)),
            pl.BlockSpec((tm, tn), lambda i, j: (i, jnp.maximum(j - d_pool // tn, 0))),
        ],
        out_shape=[
            jax.ShapeDtypeStruct((m, d_pool), F32),
            jax.ShapeDtypeStruct((m, n_in - d_pool), rest_dtype),
        ],
        scratch_shapes=[pltpu.VMEM((tm, d), BF16)],
        compiler_params=_params("arbitrary", "arbitrary"),
        name="in_proj",
    )(x2, g, cs, w_in_t)


def _group_norm(o):
    mu = jnp.mean(o, axis=-1, keepdims=True)
    d = o - mu
    var = jnp.mean(d * d, axis=-1, keepdims=True)
    return d * lax.rsqrt(var + EPS)


def _decay_tables(c):
    log_g = jnp.log(1.0 - 2.0 ** (-5.0 - jnp.arange(N_HEADS, dtype=F32)))
    idx = jnp.arange(c, dtype=F32)
    rel = idx[:, None] - idx[None, :]
    dmask = jnp.where(rel >= 0, jnp.exp(jnp.maximum(rel, 0.0)[None] * log_g[:, None, None]), 0.0)
    xi = jnp.exp((idx + 1.0)[None, :] * log_g[:, None])
    zeta = jnp.exp((c - 1.0 - idx)[None, :] * log_g[:, None])
    g_c = jnp.exp(c * log_g)
    xi_b = jnp.broadcast_to(xi[:, :, None], (N_HEADS, c, DV))
    zeta_b = jnp.broadcast_to(zeta[:, :, None], (N_HEADS, c, DK))
    gc_b = jnp.broadcast_to(g_c[:, None, None], (N_HEADS, 1, DV))
    return dmask, xi_b, zeta_b, gc_b


def _mixer_prompt_kernel(u_ref, q_ref, k_ref, v_ref, sr_ref, sa_ref, sg_ref,
                         wp_ref, ps_ref, gn_ref, dm_ref, xi_ref, zt_ref, gc_ref,
                         m_ref, rout_ref, ext_scr, r_scr):
    c = pl.program_id(1)
    rb = u_ref.shape[0]
    halo = POOL_MAX

    @pl.when(c == 0)
    def _():
        ext_scr[0:halo, :] = jnp.zeros((halo, ext_scr.shape[1]), F32)
        r_scr[...] = jnp.zeros(r_scr.shape, F32)

    ext_scr[halo:halo + rb, :] = u_ref[...]

    pg = u_ref.shape[1] // len(POOL_WINDOWS)
    pos = c * rb + lax.broadcasted_iota(jnp.int32, (rb, pg), 0)
    a_groups = []
    for g, w in enumerate(POOL_WINDOWS):
        cols = slice(g * pg, (g + 1) * pg)
        wsum = ext_scr[halo:halo + rb, cols]
        for lag in range(1, w):
            wsum = wsum + ext_scr[halo - lag:halo - lag + rb, cols]
        cnt = jnp.minimum(w, pos + 1).astype(F32)
        z = wsum / cnt - ext_scr[halo:halo + rb, cols]
        a_groups.append(jnp.dot(z.astype(BF16), wp_ref[g], preferred_element_type=F32))
    og = a_groups[0].shape[1]

    ext_scr[0:halo, :] = ext_scr[rb:rb + halo, :]

    for h in range(N_HEADS):
        qh = q_ref[:, h * DK:(h + 1) * DK]
        kh = k_ref[:, h * DK:(h + 1) * DK]
        vh = v_ref[:, h * DV:(h + 1) * DV]
        r_old = r_scr[h]
        s = lax.dot_general(qh, kh, (((1,), (1,)), ((), ())), preferred_element_type=F32) * dm_ref[h]
        inter = jnp.dot(qh, r_old.astype(BF16), preferred_element_type=F32) * xi_ref[h]
        o = jnp.dot(s.astype(BF16), vh, preferred_element_type=F32) + inter
        kz = (kh.astype(F32) * zt_ref[h]).astype(BF16)
        r_scr[h] = r_old * gc_ref[h] + lax.dot_general(
            kz, vh, (((0,), (0,)), ((), ())), preferred_element_type=F32)

        hc = slice(h * DV, (h + 1) * DV)
        on = _group_norm(o) * gn_ref[:, hc]
        r = sr_ref[:, hc].astype(F32) * on
        g, part = divmod(h * DV, og)
        a = a_groups[g][:, part:part + DV] * ps_ref[:, hc]
        mh = sa_ref[:, hc].astype(F32) * a + sg_ref[:, hc].astype(F32) * r
        m_ref[:, hc] = mh.astype(m_ref.dtype)

    @pl.when(c == pl.num_programs(1) - 1)
    def _():
        rout_ref[0] = r_scr[...]


def _mixer_prompt(u, rest, w_pool_b, pool_scale, gn_gain, *, batch, seq):
    m_rows, d_pool = u.shape
    d_v = N_HEADS * DV
    rb = RET_CHUNK
    assert seq % rb == 0 and m_rows == batch * seq
    nc = seq // rb
    dmask, xi_b, zeta_b, gc_b = _decay_tables(rb)
    d_qk = N_HEADS * DK
    row = lambda b, c: b * nc + c
    full = lambda shape: pl.BlockSpec(shape, lambda b, c: (0,) * len(shape))
    return pl.pallas_call(
        _mixer_prompt_kernel,
        grid=(batch, nc),
        in_specs=[
            pl.BlockSpec((rb, d_pool), lambda b, c: (row(b, c), 0)),
            pl.BlockSpec((rb, d_qk), lambda b, c: (row(b, c), 0)),
            pl.BlockSpec((rb, d_qk), lambda b, c: (row(b, c), 1)),
            pl.BlockSpec((rb, d_v), lambda b, c: (row(b, c), 1)),
            pl.BlockSpec((rb, d_v), lambda b, c: (row(b, c), 2)),
            pl.BlockSpec((rb, d_v), lambda b, c: (row(b, c), 3)),
            pl.BlockSpec((rb, d_v), lambda b, c: (row(b, c), 4)),
            full(w_pool_b.shape), full(pool_scale.shape), full(gn_gain.shape),
            full(dmask.shape), full(xi_b.shape), full(zeta_b.shape), full(gc_b.shape),
        ],
        out_specs=[
            pl.BlockSpec((rb, d_v), lambda b, c: (row(b, c), 0)),
            pl.BlockSpec((1, N_HEADS, DK, DV), lambda b, c: (b, 0, 0, 0)),
        ],
        out_shape=[
            jax.ShapeDtypeStruct((m_rows, d_v), BF16),
            jax.ShapeDtypeStruct((batch, N_HEADS, DK, DV), F32),
        ],
        scratch_shapes=[pltpu.VMEM((rb + POOL_MAX, d_pool), F32), pltpu.VMEM((N_HEADS, DK, DV), F32)],
        compiler_params=_params("arbitrary", "arbitrary"),
        name="mixer_prompt",
    )(u, rest, rest, rest, rest, rest, rest, w_pool_b, pool_scale, gn_gain, dmask, xi_b, zeta_b, gc_b)


def _mixer_sample_kernel(u_ref, sp_ref, q_ref, k_ref, v_ref, sr_ref, sa_ref, sg_ref, rin_ref,
                         wp_ref, ps_ref, gn_ref, dm_ref, xi_ref, zt_ref, gc_ref,
                         m_ref, rout_ref, ext_scr, *, pos0):
    bb, ln, d_pool = u_ref.shape
    hist = POOL_MAX - 1
    ext_scr[:, 1:1 + hist, :] = sp_ref[...]
    ext_scr[:, POOL_MAX:POOL_MAX + ln, :] = u_ref[...]

    pg = d_pool // len(POOL_WINDOWS)
    pos = pos0 + lax.broadcasted_iota(jnp.int32, (bb, ln, pg), 1)
    a_groups = []
    for g, w in enumerate(POOL_WINDOWS):
        cols = slice(g * pg, (g + 1) * pg)
        wsum = ext_scr[:, POOL_MAX:POOL_MAX + ln, cols]
        for lag in range(1, w):
            wsum = wsum + ext_scr[:, POOL_MAX - lag:POOL_MAX - lag + ln, cols]
        cnt = jnp.minimum(w, pos + 1).astype(F32)
        z = wsum / cnt - ext_scr[:, POOL_MAX:POOL_MAX + ln, cols]
        ag = jnp.dot(z.reshape(bb * ln, pg).astype(BF16), wp_ref[g], preferred_element_type=F32)
        a_groups.append(ag.reshape(bb, ln, ag.shape[-1]))
    og = a_groups[0].shape[-1]

    for h in range(N_HEADS):
        qh = q_ref[:, :, h * DK:(h + 1) * DK].astype(BF16)
        kf = k_ref[:, :, h * DK:(h + 1) * DK].astype(F32)
        kh = kf.astype(BF16)
        vh = v_ref[:, :, h * DV:(h + 1) * DV].astype(BF16)
        r_old = rin_ref[:, h]
        s = jnp.einsum("btd,bsd->bts", qh, kh, preferred_element_type=F32) * dm_ref[h]
        inter = jnp.einsum("btd,bdv->btv", qh, r_old.astype(BF16), preferred_element_type=F32) * xi_ref[h]
        o = jnp.einsum("bts,bsv->btv", s.astype(BF16), vh, preferred_element_type=F32) + inter
        kz_t = jnp.swapaxes(kf * zt_ref[h], 1, 2).astype(BF16)
        rout_ref[:, h] = r_old * gc_ref[h] + jnp.einsum(
            "bds,bsv->bdv", kz_t, vh, preferred_element_type=F32)

        hc = slice(h * DV, (h + 1) * DV)
        on = _group_norm(o) * gn_ref[:, hc]
        r = sr_ref[:, :, hc].astype(F32) * on
        g, part = divmod(h * DV, og)
        a = a_groups[g][:, :, part:part + DV] * ps_ref[:, hc]
        mh = sa_ref[:, :, hc].astype(F32) * a + sg_ref[:, :, hc].astype(F32) * r
        m_ref[:, :, hc] = mh.astype(m_ref.dtype)


def _mixer_sample(u, rest, state_pool, state_ret, w_pool_b, pool_scale, gn_gain, *, batch, seq, pos0, bb):
    d_pool = u.shape[-1]
    d_v, d_qk = N_HEADS * DV, N_HEADS * DK
    assert batch % bb == 0
    dmask, xi_b, zeta_b, gc_b = _decay_tables(seq)
    u3 = u.reshape(batch, seq, d_pool)
    rest3 = rest.reshape(batch, seq, rest.shape[-1])
    full = lambda shape: pl.BlockSpec(shape, lambda b: (0,) * len(shape))
    m3, r_new = pl.pallas_call(
        functools.partial(_mixer_sample_kernel, pos0=pos0),
        grid=(batch // bb,),
        in_specs=[
            pl.BlockSpec((bb, seq, d_pool), lambda b: (b, 0, 0)),
            pl.BlockSpec((bb, POOL_MAX - 1, d_pool), lambda b: (b, 0, 0)),
            pl.BlockSpec((bb, seq, d_qk), lambda b: (b, 0, 0)),
            pl.BlockSpec((bb, seq, d_qk), lambda b: (b, 0, 1)),
            pl.BlockSpec((bb, seq, d_v), lambda b: (b, 0, 1)),
            pl.BlockSpec((bb, seq, d_v), lambda b: (b, 0, 2)),
            pl.BlockSpec((bb, seq, d_v), lambda b: (b, 0, 3)),
            pl.BlockSpec((bb, seq, d_v), lambda b: (b, 0, 4)),
            pl.BlockSpec((bb, N_HEADS, DK, DV), lambda b: (b, 0, 0, 0)),
            full(w_pool_b.shape), full(pool_scale.shape), full(gn_gain.shape),
            full(dmask.shape), full(xi_b.shape), full(zeta_b.shape), full(gc_b.shape),
        ],
        out_specs=[
            pl.BlockSpec((bb, seq, d_v), lambda b: (b, 0, 0)),
            pl.BlockSpec((bb, N_HEADS, DK, DV), lambda b: (b, 0, 0, 0)),
        ],
        out_shape=[
            jax.ShapeDtypeStruct((batch, seq, d_v), F32),
            jax.ShapeDtypeStruct((batch, N_HEADS, DK, DV), F32),
        ],
        scratch_shapes=[pltpu.VMEM((bb, POOL_MAX + seq, d_pool), F32)],
        compiler_params=_params("arbitrary"),
        name="mixer_sample",
    )(u3, state_pool, rest3, rest3, rest3, rest3, rest3, rest3, state_ret,
      w_pool_b, pool_scale, gn_gain, dmask, xi_b, zeta_b, gc_b)
    return m3.reshape(batch * seq, d_v), r_new


def _out_proj_kernel(m_ref, w_ref, x_ref, gpost_ref, gpre_ref, x1_ref, h2_ref):
    proj = jnp.dot(m_ref[...].astype(BF16), w_ref[...], preferred_element_type=F32)
    x1 = x_ref[...] + _rms(proj, gpost_ref[...])
    x1_ref[...] = x1
    h2_ref[...] = _rms(x1, gpre_ref[...]).astype(BF16)


def _out_proj(m2, w_out_b, x2, g_post, g_pre):
    m, d = x2.shape
    tm = OUT_ROW_TILE
    assert m % tm == 0
    return pl.pallas_call(
        _out_proj_kernel,
        grid=(m // tm,),
        in_specs=[
            pl.BlockSpec((tm, m2.shape[1]), lambda i: (i, 0)),
            pl.BlockSpec(w_out_b.shape, lambda i: (0, 0)),
            pl.BlockSpec((tm, d), lambda i: (i, 0)),
            pl.BlockSpec((1, d), lambda i: (0, 0)),
            pl.BlockSpec((1, d), lambda i: (0, 0)),
        ],
        out_specs=[
            pl.BlockSpec((tm, d), lambda i: (i, 0)),
            pl.BlockSpec((tm, d), lambda i: (i, 0)),
        ],
        out_shape=[jax.ShapeDtypeStruct((m, d), F32), jax.ShapeDtypeStruct((m, d), BF16)],
        compiler_params=_params("arbitrary"),
        name="out_proj",
    )(m2, w_out_b, x2, g_post, g_pre)


def _gelu_tanh(x):
    return 0.5 * x * (1.0 + jnp.tanh(0.7978845608028654 * (x + 0.044715 * (x * x * x))))


def _conv_ffn_kernel(*refs, seq, blocks_per_seq, n_tiles):
    if seq >= FF_ROW_TILE:
        (h2_ref, wu_ref, cp_ref, wd_ref, x1_ref, g_ref,
         y_ref, tail_ref, act_even, act_odd, carry_scr) = refs
    else:
        (h2_ref, wu_ref, cp_ref, wd_ref, x1_ref, g_ref, stv_ref, stg_ref,
         y_ref, tail_ref, act_even, act_odd) = refs
    act = (act_even, act_odd)
    i = pl.program_id(0)
    j = pl.program_id(1)
    tm = h2_ref.shape[0]
    tf = wd_ref.shape[0]
    hist = CONV_K - 1
    jd = jnp.minimum(j, n_tiles - 1)

    if seq >= FF_ROW_TILE:
        @pl.when((i == 0) & (j == 0))
        def _():
            carry_scr[...] = jnp.zeros(carry_scr.shape, F32)

    rc, cc = FF_EW_ROWS, FF_EW_COLS

    def conv_long(part, up, r0, c0):
        cols = slice(part * tf + c0, part * tf + c0 + cc)
        cur = up[r0:r0 + rc, cols]
        if r0 == 0:
            halo = jnp.where(i % blocks_per_seq == 0, 0.0, carry_scr[jd, part, :, c0:c0 + cc])
        else:
            halo = up[r0 - 8:r0, cols]
        ext = jnp.concatenate([halo, cur], axis=0)
        out = cur * cp_ref[hist:hist + 1, cols] + cp_ref[CONV_K:CONV_K + 1, cols]
        for lag in range(1, CONV_K):
            out = out + pltpu.roll(ext, lag, axis=0)[8:] * cp_ref[hist - lag:hist - lag + 1, cols]
        return out

    def conv_short(part, up, r0, c0, st_ref):
        cols = slice(part * tf + c0, part * tf + c0 + cc)
        s0, ns = r0 // seq, rc // seq
        cur = up[r0:r0 + rc, cols].reshape(ns, seq, cc)
        t = lax.broadcasted_iota(jnp.int32, cur.shape, 1)
        out = cur * cp_ref[hist:hist + 1, cols] + cp_ref[CONV_K:CONV_K + 1, cols]
        for lag in range(1, CONV_K):
            shifted = pltpu.roll(cur, lag, axis=1)
            for r in range(lag):
                row = st_ref[s0:s0 + ns, hist - lag + r:hist - lag + r + 1, c0:c0 + cc]
                shifted = jnp.where(t == r, row, shifted)
            out = out + shifted * cp_ref[hist - lag:hist - lag + 1, cols]
        return out.reshape(rc, cc)

    def up_project():
        return jnp.dot(h2_ref[...], wu_ref[...], preferred_element_type=F32)

    def activate(up, p):
        for r0 in range(0, tm, rc):
            for c0 in range(0, tf, cc):
                if seq >= FF_ROW_TILE:
                    val, gate = conv_long(0, up, r0, c0), conv_long(1, up, r0, c0)
                else:
                    val, gate = conv_short(0, up, r0, c0, stv_ref), conv_short(1, up, r0, c0, stg_ref)
                act[p][r0:r0 + rc, c0:c0 + cc] = (_gelu_tanh(gate) * val).astype(BF16)
        for part in range(2):
            cols = slice(part * tf, (part + 1) * tf)
            if seq >= FF_ROW_TILE:
                last = up[tm - 8:tm, cols]
                carry_scr[jd, part] = last
                tail_ref[0, part] = last
            else:
                tail_ref[:, part] = up[:, cols].reshape(tm // seq, seq, tf)[:, seq - hist:, :]

    def down_project(p):
        return jnp.dot(act[p][...], wd_ref[...], preferred_element_type=F32)

    @pl.when(j == 0)
    def _():
        y_ref[...] = jnp.zeros(y_ref.shape, F32)
        activate(up_project(), 0)

    for p in (0, 1):
        @pl.when((j >= 1) & (j < n_tiles) & (j % 2 == p))
        def _():
            up = up_project()
            y_ref[...] += down_project(1 - p)
            activate(up, p)

    @pl.when(j == n_tiles)
    def _():
        acc = y_ref[...] + down_project((n_tiles - 1) % 2)
        y_ref[...] = x1_ref[...] + _rms(acc, g_ref[...])


def _tile_up_weights(w_up, conv_w, conv_b, tf):
    d, two_ff = w_up.shape
    nj = two_ff // (2 * tf)
    wu = w_up.reshape(d, 2, nj, tf).transpose(2, 0, 1, 3).reshape(nj, d, 2 * tf).astype(BF16)
    cw = conv_w.reshape(CONV_K, 2, nj, tf).transpose(2, 0, 1, 3).reshape(nj, CONV_K, 2 * tf)
    cb = conv_b.reshape(2, nj, tf).transpose(1, 0, 2).reshape(nj, 1, 2 * tf)
    pad = jnp.zeros((nj, 8 - CONV_K - 1, 2 * tf), F32)
    return wu, jnp.concatenate([cw, cb, pad], axis=1)


def _conv_ffn(h2, x1, wu_t, cp_t, w_down_b, g_post, state_conv, *, seq):
    m, d = x1.shape
    d_ff = w_down_b.shape[0]
    tm = FF_ROW_TILE
    nj, _, tf2 = wu_t.shape
    tf = tf2 // 2
    assert m % tm == 0 and nj * tf == d_ff
    hist = CONV_K - 1
    long_seq = seq >= tm
    assert nj >= 2
    up_j = lambda j: jnp.minimum(j, nj - 1)
    act_j = up_j
    down_j = lambda j: jnp.maximum(j - 1, 0)
    in_specs = [
        pl.BlockSpec((tm, d), lambda i, j: (i, 0)),
        pl.BlockSpec((pl.Squeezed(), d, 2 * tf), lambda i, j: (up_j(j), 0, 0)),
        pl.BlockSpec((pl.Squeezed(), 8, 2 * tf), lambda i, j: (act_j(j), 0, 0)),
        pl.BlockSpec((tf, d), lambda i, j: (down_j(j), 0)),
        pl.BlockSpec((tm, d), lambda i, j: (i, 0), pipeline_mode=pl.Buffered(1)),
        pl.BlockSpec((1, d), lambda i, j: (0, 0)),
    ]
    args = [h2, wu_t, cp_t, w_down_b, x1, g_post]
    scratch = [pltpu.VMEM((tm, tf), BF16), pltpu.VMEM((tm, tf), BF16)]
    if long_seq:
        assert seq % tm == 0
        blocks_per_seq = seq // tm
        tail_shape = (m // tm, 2, 8, d_ff)
        tail_spec = pl.BlockSpec((1, 2, 8, tf), lambda i, j: (i, 0, 0, act_j(j)))
        scratch += [pltpu.VMEM((nj, 2, 8, tf), F32)]
    else:
        assert tm % seq == 0 and seq == 8 and seq >= hist
        blocks_per_seq = 0
        ns = tm // seq
        in_specs += [
            pl.BlockSpec((ns, hist, tf), lambda i, j: (i, 0, act_j(j))),
            pl.BlockSpec((ns, hist, tf), lambda i, j: (i, 0, act_j(j) + nj)),
        ]
        args += [state_conv, state_conv]
        tail_shape = (m // seq, 2, hist, d_ff)
        tail_spec = pl.BlockSpec((ns, 2, hist, tf), lambda i, j: (i, 0, 0, act_j(j)))
    y, tail = pl.pallas_call(
        functools.partial(_conv_ffn_kernel, seq=seq, blocks_per_seq=blocks_per_seq, n_tiles=nj),
        grid=(m // tm, nj + 1),
        in_specs=in_specs,
        out_specs=[pl.BlockSpec((tm, d), lambda i, j: (i, 0)), tail_spec],
        out_shape=[jax.ShapeDtypeStruct((m, d), F32), jax.ShapeDtypeStruct(tail_shape, F32)],
        scratch_shapes=scratch,
        compiler_params=_params("arbitrary", "arbitrary"),
        name="conv_ffn",
    )(*args)
    if long_seq:
        tail = tail[blocks_per_seq - 1::blocks_per_seq, :, 8 - hist:, :]
    nseq = tail.shape[0]
    new_conv = jnp.swapaxes(tail, 1, 2).reshape(nseq, hist, 2 * d_ff)
    return y, new_conv


def _rotary_table(pos, rows):
    half = DK // 2
    theta = ROPE_BASE ** (-jnp.arange(half, dtype=F32) / half)
    ang = pos.astype(F32)[:, None] * theta[None, :]
    cos, sin = jnp.cos(ang), jnp.sin(ang)
    cs = jnp.concatenate([cos, cos, -sin, sin], axis=-1)
    reps = max(1, rows // cs.shape[0])
    return jnp.tile(cs, (reps, 1))


def kernel(x_prompt, x_sample, state_pool, state_ret, state_conv, g_pre_mix, w_in, w_pool, pool_scale, gn_gain,
           w_out, g_post_mix, g_pre_ffn, w_up, conv_w, conv_b, w_down, g_post_ffn):
    d = x_prompt.shape[-1]
    d_pool = state_pool.shape[-1]
    row = lambda v: v.reshape(1, -1)
    n_in = w_in.shape[1]
    w_in_t = w_in.reshape(d, n_in // IN_COL_TILE, IN_COL_TILE).transpose(1, 0, 2).astype(BF16)
    w_in_b = w_in_t
    w_pool_b, w_out_b, w_down_b = w_pool.astype(BF16), w_out.astype(BF16), w_down.astype(BF16)
    wu_t, cp_t = _tile_up_weights(w_up, conv_w, conv_b, FF_COL_TILE)

    def dense_tail(x2, m2, state_conv_path, seq):
        x1, h2 = _out_proj(m2, w_out_b, x2, row(g_post_mix), row(g_pre_ffn))
        return _conv_ffn(h2, x1, wu_t, cp_t, w_down_b, row(g_post_ffn), state_conv_path, seq=seq)

    bp, lp, _ = x_prompt.shape
    xp = x_prompt.reshape(bp * lp, d)
    cs_p = _rotary_table(jnp.arange(lp, dtype=jnp.int32), IN_ROW_TILE)
    u_p, rest_p = _in_proj(xp, row(g_pre_mix), cs_p, w_in_b, d_pool=d_pool, rest_dtype=BF16)
    m_p, ret_p = _mixer_prompt(u_p, rest_p, w_pool_b, row(pool_scale), row(gn_gain), batch=bp, seq=lp)
    y_p, conv_p = dense_tail(xp, m_p, None, lp)
    pool_p = u_p.reshape(bp, lp, d_pool)[:, lp - (POOL_MAX - 1):]

    bs, ls, _ = x_sample.shape
    xs = x_sample.reshape(bs * ls, d)
    cs_s = _rotary_table(PAST_LEN + jnp.arange(ls, dtype=jnp.int32), IN_ROW_TILE)
    u_s, rest_s = _in_proj(xs, row(g_pre_mix), cs_s, w_in_b, d_pool=d_pool, rest_dtype=F32)
    m_s, ret_s = _mixer_sample(u_s, rest_s, state_pool, state_ret, w_pool_b, row(pool_scale), row(gn_gain),
                               batch=bs, seq=ls, pos0=PAST_LEN, bb=4)
    y_s, conv_s = dense_tail(xs, m_s, state_conv, ls)
    pool_s = jnp.concatenate([state_pool, u_s.reshape(bs, ls, d_pool)], axis=1)[:, -(POOL_MAX - 1):]

    return (y_p.reshape(bp, lp, d), y_s.reshape(bs, ls, d), pool_p, ret_p.astype(x_prompt.dtype), conv_p,
            pool_s, ret_s.astype(x_sample.dtype), conv_s)
```

```python
import functools

import jax
import jax.numpy as jnp
from jax import lax
from jax.experimental import pallas as pl
from jax.experimental.pallas import tpu as pltpu

F32 = jnp.float32
BF16 = jnp.bfloat16

EPS = 1e-6
N_HEADS = 8
DK = 128
DV = 256
POOL_WINDOWS = (2, 4, 8, 16)
POOL_MAX = 16
CONV_K = 3
ROPE_BASE = 10000.0
PAST_LEN = 16384
RET_CHUNK = 128

VMEM_LIMIT_BYTES = 56 * 1024 * 1024
OUT_ROW_TILE = 512
IN_ROW_TILE = 1024
FF_ROW_TILE = 1024
IN_COL_TILE = 1024
FF_COL_TILE = 512
MIX_ROWS = 512
MIX_SEQS = 8


def _params(*sem):
    return pltpu.CompilerParams(dimension_semantics=sem, vmem_limit_bytes=VMEM_LIMIT_BYTES)


def _rms(x, g):
    return x * lax.rsqrt(jnp.mean(x * x, axis=-1, keepdims=True) + EPS) * g


def _in_proj_kernel(x_ref, g_ref, cs_ref, w_ref, u_ref, p_ref, h_scr, *, kinds):
    j = pl.program_id(1)

    @pl.when(j == 0)
    def _():
        h_scr[...] = _rms(x_ref[...], g_ref[...]).astype(BF16)

    def proj():
        return jnp.dot(h_scr[...], w_ref[...], preferred_element_type=F32)

    def rotary(acc, scale):
        cos = cs_ref[:, :DK]
        sin = cs_ref[:, DK:]
        for h in range(acc.shape[1] // DK):
            xh = acc[:, h * DK:(h + 1) * DK]
            rot = xh * cos + pltpu.roll(xh, DK // 2, axis=1) * sin
            if scale != 1.0:
                rot = rot * scale
            p_ref[:, h * DK:(h + 1) * DK] = rot.astype(p_ref.dtype)

    def tiles_of(kind):
        return [t for t, k in enumerate(kinds) if k == kind]

    def cond_of(kind):
        ts = tiles_of(kind)
        return (j >= ts[0]) & (j <= ts[-1])

    @pl.when(cond_of("u"))
    def _():
        u_ref[...] = proj()

    @pl.when(cond_of("q"))
    def _():
        rotary(proj(), 1.0)

    @pl.when(cond_of("k"))
    def _():
        rotary(proj(), DK ** -0.5)

    @pl.when(cond_of("v"))
    def _():
        p_ref[...] = proj().astype(p_ref.dtype)

    @pl.when(cond_of("silu"))
    def _():
        acc = proj()
        p_ref[...] = (acc * jax.nn.sigmoid(acc)).astype(p_ref.dtype)

    @pl.when(cond_of("sig"))
    def _():
        p_ref[...] = jax.nn.sigmoid(proj()).astype(p_ref.dtype)


def _in_proj(x2, g, cs, w_in_b, *, d_pool, rest_dtype):
    m, d = x2.shape
    n_in = w_in_b.shape[1]
    tm, tn = IN_ROW_TILE, IN_COL_TILE
    assert m % tm == 0 and n_in % tn == 0 and d_pool == tn and cs.shape[0] % tm == 0
    d_qk, d_v = N_HEADS * DK, N_HEADS * DV
    kinds = (["u"] * (d_pool // tn) + ["q"] * (d_qk // tn) + ["k"] * (d_qk // tn) + ["v"] * (d_v // tn)
             + ["silu"] * (d_v // tn) + ["sig"] * (2 * d // tn))
    assert len(kinds) == n_in // tn
    cs_blocks = cs.shape[0] // tm
    return pl.pallas_call(
        functools.partial(_in_proj_kernel, kinds=tuple(kinds)),
        grid=(m // tm, n_in // tn),
        in_specs=[
            pl.BlockSpec((tm, d), lambda i, j: (i, 0)),
            pl.BlockSpec((1, d), lambda i, j: (0, 0)),
            pl.BlockSpec((tm, 2 * DK), lambda i, j: (i % cs_blocks, 0)),
            pl.BlockSpec((d, tn), lambda i, j: (0, j)),
        ],
        out_specs=[
            pl.BlockSpec((tm, d_pool), lambda i, j: (i, 0)),
            pl.BlockSpec((tm, tn), lambda i, j: (i, jnp.maximum(j - d_pool // tn, 0))),
        ],
        out_shape=[
            jax.ShapeDtypeStruct((m, d_pool), F32),
            jax.ShapeDtypeStruct((m, n_in - d_pool), rest_dtype),
        ],
        scratch_shapes=[pltpu.VMEM((tm, d), BF16)],
        compiler_params=_params("arbitrary", "arbitrary"),
        name="in_proj",
    )(x2, g, cs, w_in_b)


def _group_norm(o):
    mu = jnp.mean(o, axis=-1, keepdims=True)
    d = o - mu
    var = jnp.mean(d * d, axis=-1, keepdims=True)
    return d * lax.rsqrt(var + EPS)


def _decay_tables(c, xi_width):
    log_g = jnp.log(1.0 - 2.0 ** (-5.0 - jnp.arange(N_HEADS, dtype=F32)))
    idx = jnp.arange(c, dtype=F32)
    rel = idx[:, None] - idx[None, :]
    dmask = jnp.where(rel >= 0, jnp.exp(jnp.maximum(rel, 0.0)[None] * log_g[:, None, None]), 0.0)
    xi = jnp.exp((idx + 1.0)[None, :] * log_g[:, None])
    zeta = jnp.exp((c - 1.0 - idx)[None, :] * log_g[:, None])
    g_c = jnp.exp(c * log_g)
    xi_b = jnp.broadcast_to(xi[:, :, None], (N_HEADS, c, xi_width))
    zeta_b = jnp.broadcast_to(zeta[:, :, None], (N_HEADS, c, DK))
    gc_b = jnp.broadcast_to(g_c[:, None, None], (N_HEADS, 1, DV))
    return dmask, xi_b, zeta_b, gc_b


def _mixer_prompt_kernel(u_ref, q_ref, k_ref, v_ref, sr_ref, sa_ref, sg_ref,
                         wp_ref, ps_ref, gn_ref, dm_ref, xi_ref, zt_ref, gc_ref,
                         m_ref, rout_ref, ext_scr, r_scr):
    c = pl.program_id(1)
    rb = u_ref.shape[0]
    halo = POOL_MAX

    @pl.when(c == 0)
    def _():
        ext_scr[0:halo, :] = jnp.zeros((halo, ext_scr.shape[1]), F32)
        r_scr[...] = jnp.zeros(r_scr.shape, F32)

    ext_scr[halo:halo + rb, :] = u_ref[...]

    pg = u_ref.shape[1] // len(POOL_WINDOWS)
    pos = c * rb + lax.broadcasted_iota(jnp.int32, (rb, pg), 0)
    a_groups = []
    for g, w in enumerate(POOL_WINDOWS):
        cols = slice(g * pg, (g + 1) * pg)
        ext = ext_scr[:, cols]
        wsum = ext
        span = 1
        while span < w:
            wsum = wsum + pltpu.roll(wsum, span, axis=0)
            span *= 2
        cnt = jnp.minimum(w, pos + 1).astype(F32)
        z = wsum[halo:] / cnt - ext[halo:]
        a_groups.append(jnp.dot(z.astype(BF16), wp_ref[g], preferred_element_type=F32))
    og = a_groups[0].shape[1]

    ext_scr[0:halo, :] = ext_scr[rb:rb + halo, :]

    for r0 in range(0, rb, RET_CHUNK):
        rows = slice(r0, r0 + RET_CHUNK)
        for h in range(N_HEADS):
            qh = q_ref[rows, h * DK:(h + 1) * DK]
            kh = k_ref[rows, h * DK:(h + 1) * DK]
            vh = v_ref[rows, h * DV:(h + 1) * DV]
            r_old = r_scr[h]
            s = lax.dot_general(qh, kh, (((1,), (1,)), ((), ())), preferred_element_type=F32) * dm_ref[h]
            inter = jnp.dot(qh, r_old.astype(BF16), preferred_element_type=F32) * xi_ref[h]
            o = jnp.dot(s.astype(BF16), vh, preferred_element_type=F32) + inter
            kz = (kh.astype(F32) * zt_ref[h]).astype(BF16)
            r_scr[h] = r_old * gc_ref[h] + lax.dot_general(
                kz, vh, (((0,), (0,)), ((), ())), preferred_element_type=F32)

            hc = slice(h * DV, (h + 1) * DV)
            on = _group_norm(o) * gn_ref[:, hc]
            r = sr_ref[rows, hc].astype(F32) * on
            g, part = divmod(h * DV, og)
            a = a_groups[g][rows, part:part + DV] * ps_ref[:, hc]
            mh = sa_ref[rows, hc].astype(F32) * a + sg_ref[rows, hc].astype(F32) * r
            m_ref[rows, hc] = mh.astype(m_ref.dtype)

    @pl.when(c == pl.num_programs(1) - 1)
    def _():
        rout_ref[0] = r_scr[...]


def _mixer_prompt(u, rest, w_pool_b, pool_scale, gn_gain, *, batch, seq):
    m_rows, d_pool = u.shape
    d_v = N_HEADS * DV
    rb = MIX_ROWS
    assert seq % rb == 0 and rb % RET_CHUNK == 0 and m_rows == batch * seq
    nc = seq // rb
    dmask, xi_b, zeta_b, gc_b = _decay_tables(RET_CHUNK, DV)
    d_qk = N_HEADS * DK
    row = lambda b, c: b * nc + c
    full = lambda shape: pl.BlockSpec(shape, lambda b, c: (0,) * len(shape))
    return pl.pallas_call(
        _mixer_prompt_kernel,
        grid=(batch, nc),
        in_specs=[
            pl.BlockSpec((rb, d_pool), lambda b, c: (row(b, c), 0)),
            pl.BlockSpec((rb, d_qk), lambda b, c: (row(b, c), 0)),
            pl.BlockSpec((rb, d_qk), lambda b, c: (row(b, c), 1)),
            pl.BlockSpec((rb, d_v), lambda b, c: (row(b, c), 1)),
            pl.BlockSpec((rb, d_v), lambda b, c: (row(b, c), 2)),
            pl.BlockSpec((rb, d_v), lambda b, c: (row(b, c), 3)),
            pl.BlockSpec((rb, d_v), lambda b, c: (row(b, c), 4)),
            full(w_pool_b.shape), full(pool_scale.shape), full(gn_gain.shape),
            full(dmask.shape), full(xi_b.shape), full(zeta_b.shape), full(gc_b.shape),
        ],
        out_specs=[
            pl.BlockSpec((rb, d_v), lambda b, c: (row(b, c), 0)),
            pl.BlockSpec((1, N_HEADS, DK, DV), lambda b, c: (b, 0, 0, 0)),
        ],
        out_shape=[
            jax.ShapeDtypeStruct((m_rows, d_v), BF16),
            jax.ShapeDtypeStruct((batch, N_HEADS, DK, DV), F32),
        ],
        scratch_shapes=[pltpu.VMEM((rb + POOL_MAX, d_pool), F32), pltpu.VMEM((N_HEADS, DK, DV), F32)],
        compiler_params=_params("arbitrary", "arbitrary"),
        name="mixer_prompt",
    )(u, rest, rest, rest, rest, rest, rest, w_pool_b, pool_scale, gn_gain, dmask, xi_b, zeta_b, gc_b)


def _mixer_sample_kernel(u_ref, sp_ref, q_ref, k_ref, v_ref, sr_ref, sa_ref, sg_ref, rin_ref,
                         wp_ref, ps_ref, gn_ref, dm_ref, xi_ref, zt_ref, gc_ref,
                         m_ref, rout_ref, ext_scr, *, pos0):
    bb, ln, d_pool = u_ref.shape
    hist = POOL_MAX - 1
    ext_scr[:, 1:1 + hist, :] = sp_ref[...]
    ext_scr[:, POOL_MAX:POOL_MAX + ln, :] = u_ref[...]

    pg = d_pool // len(POOL_WINDOWS)
    pos = pos0 + lax.broadcasted_iota(jnp.int32, (bb, ln, pg), 1)
    a_groups = []
    for g, w in enumerate(POOL_WINDOWS):
        cols = slice(g * pg, (g + 1) * pg)
        wsum = ext_scr[:, POOL_MAX:POOL_MAX + ln, cols]
        for lag in range(1, w):
            wsum = wsum + ext_scr[:, POOL_MAX - lag:POOL_MAX - lag + ln, cols]
        cnt = jnp.minimum(w, pos + 1).astype(F32)
        z = wsum / cnt - ext_scr[:, POOL_MAX:POOL_MAX + ln, cols]
        ag = jnp.dot(z.reshape(bb * ln, pg).astype(BF16), wp_ref[g], preferred_element_type=F32)
        a_groups.append(ag.reshape(bb, ln, ag.shape[-1]))
    og = a_groups[0].shape[-1]

    for h in range(N_HEADS):
        qh = q_ref[:, :, h * DK:(h + 1) * DK].astype(BF16)
        kf = k_ref[:, :, h * DK:(h + 1) * DK].astype(F32)
        kh = kf.astype(BF16)
        vh = v_ref[:, :, h * DV:(h + 1) * DV].astype(BF16)
        r_old = rin_ref[:, h]
        s = jnp.einsum("btd,bsd->bts", qh, kh, preferred_element_type=F32) * dm_ref[h]
        inter = jnp.einsum("btd,bdv->btv", qh, r_old.astype(BF16), preferred_element_type=F32) * xi_ref[h]
        o = jnp.einsum("bts,bsv->btv", s.astype(BF16), vh, preferred_element_type=F32) + inter
        kz_t = jnp.swapaxes(kf * zt_ref[h], 1, 2).astype(BF16)
        rout_ref[:, h] = r_old * gc_ref[h] + jnp.einsum(
            "bds,bsv->bdv", kz_t, vh, preferred_element_type=F32)

        hc = slice(h * DV, (h + 1) * DV)
        on = _group_norm(o) * gn_ref[:, hc]
        r = sr_ref[:, :, hc].astype(F32) * on
        g, part = divmod(h * DV, og)
        a = a_groups[g][:, :, part:part + DV] * ps_ref[:, hc]
        mh = sa_ref[:, :, hc].astype(F32) * a + sg_ref[:, :, hc].astype(F32) * r
        m_ref[:, :, hc] = mh.astype(m_ref.dtype)


def _mixer_sample(u, rest, state_pool, state_ret, w_pool_b, pool_scale, gn_gain, *, batch, seq, pos0):
    d_pool = u.shape[-1]
    d_v, d_qk = N_HEADS * DV, N_HEADS * DK
    bb = MIX_SEQS
    assert batch % bb == 0
    dmask, xi_b, zeta_b, gc_b = _decay_tables(seq, DV)
    u3 = u.reshape(batch, seq, d_pool)
    rest3 = rest.reshape(batch, seq, rest.shape[-1])
    full = lambda shape: pl.BlockSpec(shape, lambda b: (0,) * len(shape))
    m3, r_new = pl.pallas_call(
        functools.partial(_mixer_sample_kernel, pos0=pos0),
        grid=(batch // bb,),
        in_specs=[
            pl.BlockSpec((bb, seq, d_pool), lambda b: (b, 0, 0)),
            pl.BlockSpec((bb, POOL_MAX - 1, d_pool), lambda b: (b, 0, 0)),
            pl.BlockSpec((bb, seq, d_qk), lambda b: (b, 0, 0)),
            pl.BlockSpec((bb, seq, d_qk), lambda b: (b, 0, 1)),
            pl.BlockSpec((bb, seq, d_v), lambda b: (b, 0, 1)),
            pl.BlockSpec((bb, seq, d_v), lambda b: (b, 0, 2)),
            pl.BlockSpec((bb, seq, d_v), lambda b: (b, 0, 3)),
            pl.BlockSpec((bb, seq, d_v), lambda b: (b, 0, 4)),
            pl.BlockSpec((bb, N_HEADS, DK, DV), lambda b: (b, 0, 0, 0)),
            full(w_pool_b.shape), full(pool_scale.shape), full(gn_gain.shape),
            full(dmask.shape), full(xi_b.shape), full(zeta_b.shape), full(gc_b.shape),
        ],
        out_specs=[
            pl.BlockSpec((bb, seq, d_v), lambda b: (b, 0, 0)),
            pl.BlockSpec((bb, N_HEADS, DK, DV), lambda b: (b, 0, 0, 0)),
        ],
        out_shape=[
            jax.ShapeDtypeStruct((batch, seq, d_v), F32),
            jax.ShapeDtypeStruct((batch, N_HEADS, DK, DV), F32),
        ],
        scratch_shapes=[pltpu.VMEM((bb, POOL_MAX + seq, d_pool), F32)],
        compiler_params=_params("arbitrary"),
        name="mixer_sample",
    )(u3, state_pool, rest3, rest3, rest3, rest3, rest3, rest3, state_ret,
      w_pool_b, pool_scale, gn_gain, dmask, xi_b, zeta_b, gc_b)
    return m3.reshape(batch * seq, d_v), r_new


def _out_proj_kernel(m_ref, w_ref, x_ref, gpost_ref, gpre_ref, x1_ref, h2_ref):
    proj = jnp.dot(m_ref[...].astype(BF16), w_ref[...], preferred_element_type=F32)
    x1 = x_ref[...] + _rms(proj, gpost_ref[...])
    x1_ref[...] = x1
    h2_ref[...] = _rms(x1, gpre_ref[...]).astype(BF16)


def _out_proj(m2, w_out_b, x2, g_post, g_pre):
    m, d = x2.shape
    tm = OUT_ROW_TILE
    assert m % tm == 0
    return pl.pallas_call(
        _out_proj_kernel,
        grid=(m // tm,),
        in_specs=[
            pl.BlockSpec((tm, m2.shape[1]), lambda i: (i, 0)),
            pl.BlockSpec(w_out_b.shape, lambda i: (0, 0)),
            pl.BlockSpec((tm, d), lambda i: (i, 0)),
            pl.BlockSpec((1, d), lambda i: (0, 0)),
            pl.BlockSpec((1, d), lambda i: (0, 0)),
        ],
        out_specs=[
            pl.BlockSpec((tm, d), lambda i: (i, 0)),
            pl.BlockSpec((tm, d), lambda i: (i, 0)),
        ],
        out_shape=[jax.ShapeDtypeStruct((m, d), F32), jax.ShapeDtypeStruct((m, d), BF16)],
        compiler_params=_params("arbitrary"),
        name="out_proj",
    )(m2, w_out_b, x2, g_post, g_pre)


def _gelu_tanh(x):
    return 0.5 * x * (1.0 + jnp.tanh(0.7978845608028654 * (x + 0.044715 * (x * x * x))))


def _conv_ffn_kernel(*refs, seq, blocks_per_seq):
    if seq >= FF_ROW_TILE:
        (h2_ref, wv_ref, wg_ref, cwv_ref, cwg_ref, cbv_ref, cbg_ref, wd_ref, x1_ref, g_ref,
         y_ref, tail_ref, act_cur, act_prev, carry_scr) = refs
    else:
        (h2_ref, wv_ref, wg_ref, cwv_ref, cwg_ref, cbv_ref, cbg_ref, wd_ref, x1_ref, g_ref, stv_ref, stg_ref,
         y_ref, tail_ref, act_cur, act_prev) = refs
    i = pl.program_id(0)
    j = pl.program_id(1)
    nj = pl.num_programs(1) - 1
    tm = h2_ref.shape[0]
    hist = CONV_K - 1

    if seq >= FF_ROW_TILE:
        @pl.when((i == 0) & (j == 0))
        def _():
            carry_scr[...] = jnp.zeros(carry_scr.shape, F32)

    def conv_long(part, up, cw_ref, cb_ref):
        prev = jnp.where(i % blocks_per_seq == 0, 0.0, carry_scr[j, part])
        row8 = lax.broadcasted_iota(jnp.int32, prev.shape, 0)
        out = up * cw_ref[hist:hist + 1, :] + cb_ref[...]
        for lag in range(1, CONV_K):
            rolled = pltpu.roll(up, lag, axis=0)
            top = jnp.where(row8 < lag, pltpu.roll(prev, lag, axis=0), rolled[0:8])
            shifted = jnp.concatenate([top, rolled[8:]], axis=0)
            out = out + shifted * cw_ref[hist - lag:hist - lag + 1, :]
        last = up[tm - 8:tm]
        carry_scr[j, part] = last
        tail_ref[0, part] = last
        return out

    def conv_short(part, up, cw_ref, cb_ref, st_ref):
        ns = tm // seq
        up3 = up.reshape(ns, seq, up.shape[1])
        t = lax.broadcasted_iota(jnp.int32, up3.shape, 1)
        out = up3 * cw_ref[hist:hist + 1, :] + cb_ref[...]
        for lag in range(1, CONV_K):
            shifted = pltpu.roll(up3, lag, axis=1)
            for r in range(lag):
                row = st_ref[:, hist - lag + r:hist - lag + r + 1, :]
                shifted = jnp.where(t == r, row, shifted)
            out = out + shifted * cw_ref[hist - lag:hist - lag + 1, :]
        tail_ref[:, part] = up3[:, seq - hist:, :]
        return out.reshape(tm, up.shape[1])

    def up_phase():
        h2 = h2_ref[...]
        upv = jnp.dot(h2, wv_ref[...], preferred_element_type=F32)
        upg = jnp.dot(h2, wg_ref[...], preferred_element_type=F32)
        if seq >= FF_ROW_TILE:
            val = conv_long(0, upv, cwv_ref, cbv_ref)
            gate = conv_long(1, upg, cwg_ref, cbg_ref)
        else:
            val = conv_short(0, upv, cwv_ref, cbv_ref, stv_ref)
            gate = conv_short(1, upg, cwg_ref, cbg_ref, stg_ref)
        act_cur[...] = (_gelu_tanh(gate) * val).astype(BF16)

    @pl.when(j == 0)
    def _():
        y_ref[...] = jnp.zeros(y_ref.shape, F32)
        up_phase()

    @pl.when((j > 0) & (j < nj))
    def _():
        act_prev[...] = act_cur[...]
        up_phase()
        y_ref[...] += jnp.dot(act_prev[...], wd_ref[...], preferred_element_type=F32)

    @pl.when(j == nj)
    def _():
        acc = y_ref[...] + jnp.dot(act_cur[...], wd_ref[...], preferred_element_type=F32)
        y_ref[...] = x1_ref[...] + _rms(acc, g_ref[...])


def _conv_ffn(h2, x1, w_up_b, conv_w, conv_b, w_down_b, g_post, state_conv, *, seq):
    m, d = x1.shape
    d_ff = w_down_b.shape[0]
    tm, tf = FF_ROW_TILE, FF_COL_TILE
    assert m % tm == 0 and d_ff % tf == 0
    nj = d_ff // tf
    hist = CONV_K - 1
    conv_b2 = conv_b.reshape(1, 2 * d_ff)
    long_seq = seq >= tm
    up_j = lambda j: jnp.minimum(j, nj - 1)
    down_j = lambda j: jnp.maximum(j - 1, 0)
    in_specs = [
        pl.BlockSpec((tm, d), lambda i, j: (i, 0)),
        pl.BlockSpec((d, tf), lambda i, j: (0, up_j(j))),
        pl.BlockSpec((d, tf), lambda i, j: (0, up_j(j) + nj)),
        pl.BlockSpec((CONV_K, tf), lambda i, j: (0, up_j(j))),
        pl.BlockSpec((CONV_K, tf), lambda i, j: (0, up_j(j) + nj)),
        pl.BlockSpec((1, tf), lambda i, j: (0, up_j(j))),
        pl.BlockSpec((1, tf), lambda i, j: (0, up_j(j) + nj)),
        pl.BlockSpec((tf, d), lambda i, j: (down_j(j), 0)),
        pl.BlockSpec((tm, d), lambda i, j: (i, 0), pipeline_mode=pl.Buffered(1)),
        pl.BlockSpec((1, d), lambda i, j: (0, 0)),
    ]
    args = [h2, w_up_b, w_up_b, conv_w, conv_w, conv_b2, conv_b2, w_down_b, x1, g_post]
    scratch = [pltpu.VMEM((tm, tf), BF16), pltpu.VMEM((tm, tf), BF16)]
    if long_seq:
        assert seq % tm == 0
        blocks_per_seq = seq // tm
        tail_shape = (m // tm, 2, 8, d_ff)
        tail_spec = pl.BlockSpec((1, 2, 8, tf), lambda i, j: (i, 0, 0, up_j(j)))
        scratch += [pltpu.VMEM((nj, 2, 8, tf), F32)]
    else:
        assert tm % seq == 0 and seq == 8 and seq >= hist
        blocks_per_seq = 0
        ns = tm // seq
        in_specs += [
            pl.BlockSpec((ns, hist, tf), lambda i, j: (i, 0, up_j(j))),
            pl.BlockSpec((ns, hist, tf), lambda i, j: (i, 0, up_j(j) + nj)),
        ]
        args += [state_conv, state_conv]
        tail_shape = (m // seq, 2, hist, d_ff)
        tail_spec = pl.BlockSpec((ns, 2, hist, tf), lambda i, j: (i, 0, 0, up_j(j)))
    y, tail = pl.pallas_call(
        functools.partial(_conv_ffn_kernel, seq=seq, blocks_per_seq=blocks_per_seq),
        grid=(m // tm, nj + 1),
        in_specs=in_specs,
        out_specs=[pl.BlockSpec((tm, d), lambda i, j: (i, 0)), tail_spec],
        out_shape=[jax.ShapeDtypeStruct((m, d), F32), jax.ShapeDtypeStruct(tail_shape, F32)],
        scratch_shapes=scratch,
        compiler_params=_params("arbitrary", "arbitrary"),
        name="conv_ffn",
    )(*args)
    if long_seq:
        tail = tail[blocks_per_seq - 1::blocks_per_seq, :, 8 - hist:, :]
    nseq = tail.shape[0]
    new_conv = jnp.swapaxes(tail, 1, 2).reshape(nseq, hist, 2 * d_ff)
    return y, new_conv


def _rotary_table(pos, rows):
    half = DK // 2
    theta = ROPE_BASE ** (-jnp.arange(half, dtype=F32) / half)
    ang = pos.astype(F32)[:, None] * theta[None, :]
    cos, sin = jnp.cos(ang), jnp.sin(ang)
    cs = jnp.concatenate([cos, cos, -sin, sin], axis=-1)
    reps = max(1, rows // cs.shape[0])
    return jnp.tile(cs, (reps, 1))


def kernel(x_prompt, x_sample, state_pool, state_ret, state_conv, g_pre_mix, w_in, w_pool, pool_scale, gn_gain,
           w_out, g_post_mix, g_pre_ffn, w_up, conv_w, conv_b, w_down, g_post_ffn):
    d = x_prompt.shape[-1]
    d_pool = state_pool.shape[-1]
    row = lambda v: v.reshape(1, -1)
    w_in_b, w_pool_b, w_out_b = w_in.astype(BF16), w_pool.astype(BF16), w_out.astype(BF16)
    w_up_b, w_down_b = w_up.astype(BF16), w_down.astype(BF16)

    def dense_tail(x2, m2, state_conv_path, seq):
        x1, h2 = _out_proj(m2, w_out_b, x2, row(g_post_mix), row(g_pre_ffn))
        return _conv_ffn(h2, x1, w_up_b, conv_w, conv_b, w_down_b, row(g_post_ffn), state_conv_path, seq=seq)

    bp, lp, _ = x_prompt.shape
    xp = x_prompt.reshape(bp * lp, d)
    cs_p = _rotary_table(jnp.arange(lp, dtype=jnp.int32), IN_ROW_TILE)
    u_p, rest_p = _in_proj(xp, row(g_pre_mix), cs_p, w_in_b, d_pool=d_pool, rest_dtype=BF16)
    m_p, ret_p = _mixer_prompt(u_p, rest_p, w_pool_b, row(pool_scale), row(gn_gain), batch=bp, seq=lp)
    y_p, conv_p = dense_tail(xp, m_p, None, lp)
    pool_p = u_p.reshape(bp, lp, d_pool)[:, lp - (POOL_MAX - 1):]

    bs, ls, _ = x_sample.shape
    xs = x_sample.reshape(bs * ls, d)
    cs_s = _rotary_table(PAST_LEN + jnp.arange(ls, dtype=jnp.int32), IN_ROW_TILE)
    u_s, rest_s = _in_proj(xs, row(g_pre_mix), cs_s, w_in_b, d_pool=d_pool, rest_dtype=F32)
    m_s, ret_s = _mixer_sample(u_s, rest_s, state_pool, state_ret, w_pool_b, row(pool_scale), row(gn_gain),
                               batch=bs, seq=ls, pos0=PAST_LEN)
    y_s, conv_s = dense_tail(xs, m_s, state_conv, ls)
    pool_s = jnp.concatenate([state_pool, u_s.reshape(bs, ls, d_pool)], axis=1)[:, -(POOL_MAX - 1):]

    return (y_p.reshape(bp, lp, d), y_s.reshape(bs, ls, d), pool_p, ret_p.astype(x_prompt.dtype), conv_p,
            pool_s, ret_s.astype(x_sample.dtype), conv_s)
```

```python
import functools

import jax
import jax.numpy as jnp
from jax import lax
from jax.experimental import pallas as pl
from jax.experimental.pallas import tpu as pltpu

F32 = jnp.float32
BF16 = jnp.bfloat16

EPS = 1e-6
N_HEADS = 8
DK = 128
DV = 256
POOL_WINDOWS = (2, 4, 8, 16)
POOL_MAX = 16
CONV_K = 3
ROPE_BASE = 10000.0
PAST_LEN = 16384
RET_CHUNK = 128

VMEM_LIMIT_BYTES = 56 * 1024 * 1024
OUT_ROW_TILE = 512
IN_ROW_TILE = 1024
FF_ROW_TILE = 1024
IN_COL_TILE = 1024
FF_COL_TILE = 512
MIX_ROWS = 512
MIX_SEQS = 8
BF16_ROWS = 16


def _params(*sem):
    return pltpu.CompilerParams(dimension_semantics=sem, vmem_limit_bytes=VMEM_LIMIT_BYTES)


def _rms(x, g):
    return x * lax.rsqrt(jnp.mean(x * x, axis=-1, keepdims=True) + EPS) * g


def _in_proj_kernel(*refs, kinds, n_cast):
    x_ref, g_ref, cs_ref, w_ref = refs[:4]
    cast_src = refs[4:4 + n_cast]
    u_ref, p_ref = refs[4 + n_cast:6 + n_cast]
    cast_dst = refs[6 + n_cast:6 + 2 * n_cast]
    h_scr = refs[6 + 2 * n_cast]
    j = pl.program_id(1)

    @pl.when(j == 0)
    def _():
        h_scr[...] = _rms(x_ref[...], g_ref[...]).astype(BF16)

    def proj():
        for src, dst in zip(cast_src, cast_dst):
            dst[...] = src[...].astype(BF16)
        return jnp.dot(h_scr[...], w_ref[...], preferred_element_type=F32)

    def rotary(acc, scale):
        cos = cs_ref[:, :DK]
        sin = cs_ref[:, DK:]
        for h in range(acc.shape[1] // DK):
            xh = acc[:, h * DK:(h + 1) * DK]
            rot = xh * cos + pltpu.roll(xh, DK // 2, axis=1) * sin
            if scale != 1.0:
                rot = rot * scale
            p_ref[:, h * DK:(h + 1) * DK] = rot.astype(p_ref.dtype)

    def tiles_of(kind):
        return [t for t, k in enumerate(kinds) if k == kind]

    def cond_of(kind):
        ts = tiles_of(kind)
        return (j >= ts[0]) & (j <= ts[-1])

    @pl.when(cond_of("u"))
    def _():
        u_ref[...] = proj()

    @pl.when(cond_of("q"))
    def _():
        rotary(proj(), 1.0)

    @pl.when(cond_of("k"))
    def _():
        rotary(proj(), DK ** -0.5)

    @pl.when(cond_of("v"))
    def _():
        p_ref[...] = proj().astype(p_ref.dtype)

    def sigmoid(x):
        return 0.5 * jnp.tanh(0.5 * x) + 0.5

    @pl.when(cond_of("silu"))
    def _():
        acc = proj()
        p_ref[...] = (acc * sigmoid(acc)).astype(p_ref.dtype)

    @pl.when(cond_of("sig"))
    def _():
        p_ref[...] = sigmoid(proj()).astype(p_ref.dtype)


def _in_proj(x2, g, cs, w_in_b, *, d_pool, rest_dtype, cast=()):
    m, d = x2.shape
    n_in = w_in_b.shape[1]
    tm, tn = IN_ROW_TILE, IN_COL_TILE
    assert m % tm == 0 and n_in % tn == 0 and d_pool == tn and cs.shape[0] % tm == 0
    d_qk, d_v = N_HEADS * DK, N_HEADS * DV
    kinds = (["u"] * (d_pool // tn) + ["q"] * (d_qk // tn) + ["k"] * (d_qk // tn) + ["v"] * (d_v // tn)
             + ["silu"] * (d_v // tn) + ["sig"] * (2 * d // tn))
    nj = n_in // tn
    assert len(kinds) == nj
    cs_blocks = cs.shape[0] // tm
    steps = (m // tm) * nj
    cast_specs, cast_shapes = [], []
    for a in cast:
        rows, cols = a.shape
        rb = -(-rows // (steps * BF16_ROWS)) * BF16_ROWS
        while rows % rb:
            rb += BF16_ROWS
        nblk = rows // rb
        cast_specs.append(pl.BlockSpec((rb, cols), lambda i, j, nblk=nblk: (jnp.minimum(i * nj + j, nblk - 1), 0)))
        cast_shapes.append(jax.ShapeDtypeStruct((rows, cols), BF16))
    return pl.pallas_call(
        functools.partial(_in_proj_kernel, kinds=tuple(kinds), n_cast=len(cast)),
        grid=(m // tm, nj),
        in_specs=[
            pl.BlockSpec((tm, d), lambda i, j: (i, 0)),
            pl.BlockSpec((1, d), lambda i, j: (0, 0)),
            pl.BlockSpec((tm, 2 * DK), lambda i, j: (i % cs_blocks, 0)),
            pl.BlockSpec((d, tn), lambda i, j: (0, j)),
        ] + cast_specs,
        out_specs=[
            pl.BlockSpec((tm, d_pool), lambda i, j: (i, 0)),
            pl.BlockSpec((tm, tn), lambda i, j: (i, jnp.maximum(j - d_pool // tn, 0))),
        ] + cast_specs,
        out_shape=[
            jax.ShapeDtypeStruct((m, d_pool), F32),
            jax.ShapeDtypeStruct((m, n_in - d_pool), rest_dtype),
        ] + cast_shapes,
        scratch_shapes=[pltpu.VMEM((tm, d), BF16)],
        compiler_params=_params("arbitrary", "arbitrary"),
        name="in_proj",
    )(x2, g, cs, w_in_b, *cast)


def _group_norm(o):
    mu = jnp.mean(o, axis=-1, keepdims=True)
    d = o - mu
    var = jnp.mean(d * d, axis=-1, keepdims=True)
    return d * lax.rsqrt(var + EPS)


def _decay_tables(c, xi_width):
    log_g = jnp.log(1.0 - 2.0 ** (-5.0 - jnp.arange(N_HEADS, dtype=F32)))
    idx = jnp.arange(c, dtype=F32)
    rel = idx[:, None] - idx[None, :]
    dmask = jnp.where(rel >= 0, jnp.exp(jnp.maximum(rel, 0.0)[None] * log_g[:, None, None]), 0.0)
    xi = jnp.exp((idx + 1.0)[None, :] * log_g[:, None])
    zeta = jnp.exp((c - 1.0 - idx)[None, :] * log_g[:, None])
    g_c = jnp.exp(c * log_g)
    xi_b = jnp.broadcast_to(xi[:, :, None], (N_HEADS, c, xi_width))
    zeta_b = jnp.broadcast_to(zeta[:, :, None], (N_HEADS, c, DK))
    gc_b = jnp.broadcast_to(g_c[:, None, None], (N_HEADS, 1, DV))
    return dmask, xi_b, zeta_b, gc_b


def _mixer_prompt_kernel(u_ref, q_ref, k_ref, v_ref, sr_ref, sa_ref, sg_ref,
                         wp_ref, ps_ref, gn_ref, dm_ref, xi_ref, zt_ref, gc_ref,
                         m_ref, rout_ref, ext_scr, r_scr):
    c = pl.program_id(1)
    rb = u_ref.shape[0]
    halo = POOL_MAX

    @pl.when(c == 0)
    def _():
        ext_scr[0:halo, :] = jnp.zeros((halo, ext_scr.shape[1]), F32)
        r_scr[...] = jnp.zeros(r_scr.shape, F32)

    ext_scr[halo:halo + rb, :] = u_ref[...]

    pg = u_ref.shape[1] // len(POOL_WINDOWS)
    pos = c * rb + lax.broadcasted_iota(jnp.int32, (rb, pg), 0)
    a_groups = []
    for g, w in enumerate(POOL_WINDOWS):
        cols = slice(g * pg, (g + 1) * pg)
        ext = ext_scr[:, cols]
        wsum = ext
        span = 1
        while span < w:
            wsum = wsum + pltpu.roll(wsum, span, axis=0)
            span *= 2
        cnt = jnp.minimum(w, pos + 1).astype(F32)
        z = wsum[halo:] / cnt - ext[halo:]
        a_groups.append(jnp.dot(z.astype(BF16), wp_ref[g], preferred_element_type=F32))
    og = a_groups[0].shape[1]

    ext_scr[0:halo, :] = ext_scr[rb:rb + halo, :]

    for r0 in range(0, rb, RET_CHUNK):
        rows = slice(r0, r0 + RET_CHUNK)
        for h in range(N_HEADS):
            qh = q_ref[rows, h * DK:(h + 1) * DK]
            kh = k_ref[rows, h * DK:(h + 1) * DK]
            vh = v_ref[rows, h * DV:(h + 1) * DV]
            r_old = r_scr[h]
            s = lax.dot_general(qh, kh, (((1,), (1,)), ((), ())), preferred_element_type=F32) * dm_ref[h]
            inter = jnp.dot(qh, r_old.astype(BF16), preferred_element_type=F32) * xi_ref[h]
            o = jnp.dot(s.astype(BF16), vh, preferred_element_type=F32) + inter
            kz = (kh.astype(F32) * zt_ref[h]).astype(BF16)
            r_scr[h] = r_old * gc_ref[h] + lax.dot_general(
                kz, vh, (((0,), (0,)), ((), ())), preferred_element_type=F32)

            hc = slice(h * DV, (h + 1) * DV)
            on = _group_norm(o) * gn_ref[:, hc]
            r = sr_ref[rows, hc].astype(F32) * on
            g, part = divmod(h * DV, og)
            a = a_groups[g][rows, part:part + DV] * ps_ref[:, hc]
            mh = sa_ref[rows, hc].astype(F32) * a + sg_ref[rows, hc].astype(F32) * r
            m_ref[rows, hc] = mh.astype(m_ref.dtype)

    @pl.when(c == pl.num_programs(1) - 1)
    def _():
        rout_ref[0] = r_scr[...]


def _mixer_prompt(u, rest, w_pool_b, pool_scale, gn_gain, *, batch, seq):
    m_rows, d_pool = u.shape
    d_v = N_HEADS * DV
    rb = MIX_ROWS
    assert seq % rb == 0 and rb % RET_CHUNK == 0 and m_rows == batch * seq
    nc = seq // rb
    dmask, xi_b, zeta_b, gc_b = _decay_tables(RET_CHUNK, DV)
    d_qk = N_HEADS * DK
    row = lambda b, c: b * nc + c
    full = lambda shape: pl.BlockSpec(shape, lambda b, c: (0,) * len(shape))
    return pl.pallas_call(
        _mixer_prompt_kernel,
        grid=(batch, nc),
        in_specs=[
            pl.BlockSpec((rb, d_pool), lambda b, c: (row(b, c), 0)),
            pl.BlockSpec((rb, d_qk), lambda b, c: (row(b, c), 0)),
            pl.BlockSpec((rb, d_qk), lambda b, c: (row(b, c), 1)),
            pl.BlockSpec((rb, d_v), lambda b, c: (row(b, c), 1)),
            pl.BlockSpec((rb, d_v), lambda b, c: (row(b, c), 2)),
            pl.BlockSpec((rb, d_v), lambda b, c: (row(b, c), 3)),
            pl.BlockSpec((rb, d_v), lambda b, c: (row(b, c), 4)),
            full(w_pool_b.shape), full(pool_scale.shape), full(gn_gain.shape),
            full(dmask.shape), full(xi_b.shape), full(zeta_b.shape), full(gc_b.shape),
        ],
        out_specs=[
            pl.BlockSpec((rb, d_v), lambda b, c: (row(b, c), 0)),
            pl.BlockSpec((1, N_HEADS, DK, DV), lambda b, c: (b, 0, 0, 0)),
        ],
        out_shape=[
            jax.ShapeDtypeStruct((m_rows, d_v), BF16),
            jax.ShapeDtypeStruct((batch, N_HEADS, DK, DV), F32),
        ],
        scratch_shapes=[pltpu.VMEM((rb + POOL_MAX, d_pool), F32), pltpu.VMEM((N_HEADS, DK, DV), F32)],
        compiler_params=_params("arbitrary", "arbitrary"),
        name="mixer_prompt",
    )(u, rest, rest, rest, rest, rest, rest, w_pool_b, pool_scale, gn_gain, dmask, xi_b, zeta_b, gc_b)


def _mixer_sample_kernel(u_ref, sp_ref, q_ref, k_ref, v_ref, sr_ref, sa_ref, sg_ref, rin_ref,
                         wp_ref, ps_ref, gn_ref, dm_ref, xi_ref, zt_ref, gc_ref,
                         m_ref, rout_ref, ext_scr, *, pos0):
    bb, ln, d_pool = u_ref.shape
    hist = POOL_MAX - 1
    ext_scr[:, 1:1 + hist, :] = sp_ref[...]
    ext_scr[:, POOL_MAX:POOL_MAX + ln, :] = u_ref[...]

    pg = d_pool // len(POOL_WINDOWS)
    pos = pos0 + lax.broadcasted_iota(jnp.int32, (bb, ln, pg), 1)
    a_groups = []
    for g, w in enumerate(POOL_WINDOWS):
        cols = slice(g * pg, (g + 1) * pg)
        wsum = ext_scr[:, POOL_MAX:POOL_MAX + ln, cols]
        for lag in range(1, w):
            wsum = wsum + ext_scr[:, POOL_MAX - lag:POOL_MAX - lag + ln, cols]
        cnt = jnp.minimum(w, pos + 1).astype(F32)
        z = wsum / cnt - ext_scr[:, POOL_MAX:POOL_MAX + ln, cols]
        ag = jnp.dot(z.reshape(bb * ln, pg).astype(BF16), wp_ref[g], preferred_element_type=F32)
        a_groups.append(ag.reshape(bb, ln, ag.shape[-1]))
    og = a_groups[0].shape[-1]

    for h in range(N_HEADS):
        qh = q_ref[:, :, h * DK:(h + 1) * DK].astype(BF16)
        kf = k_ref[:, :, h * DK:(h + 1) * DK].astype(F32)
        kh = kf.astype(BF16)
        vh = v_ref[:, :, h * DV:(h + 1) * DV].astype(BF16)
        r_old = rin_ref[:, h]
        s = jnp.einsum("btd,bsd->bts", qh, kh, preferred_element_type=F32) * dm_ref[h]
        inter = jnp.einsum("btd,bdv->btv", qh, r_old.astype(BF16), preferred_element_type=F32) * xi_ref[h]
        o = jnp.einsum("bts,bsv->btv", s.astype(BF16), vh, preferred_element_type=F32) + inter
        kz_t = jnp.swapaxes(kf * zt_ref[h], 1, 2).astype(BF16)
        rout_ref[:, h] = r_old * gc_ref[h] + jnp.einsum(
            "bds,bsv->bdv", kz_t, vh, preferred_element_type=F32)

        hc = slice(h * DV, (h + 1) * DV)
        on = _group_norm(o) * gn_ref[:, hc]
        r = sr_ref[:, :, hc].astype(F32) * on
        g, part = divmod(h * DV, og)
        a = a_groups[g][:, :, part:part + DV] * ps_ref[:, hc]
        mh = sa_ref[:, :, hc].astype(F32) * a + sg_ref[:, :, hc].astype(F32) * r
        m_ref[:, :, hc] = mh.astype(m_ref.dtype)


def _mixer_sample(u, rest, state_pool, state_ret, w_pool_b, pool_scale, gn_gain, *, batch, seq, pos0):
    d_pool = u.shape[-1]
    d_v, d_qk = N_HEADS * DV, N_HEADS * DK
    bb = MIX_SEQS
    assert batch % bb == 0
    dmask, xi_b, zeta_b, gc_b = _decay_tables(seq, DV)
    u3 = u.reshape(batch, seq, d_pool)
    rest3 = rest.reshape(batch, seq, rest.shape[-1])
    full = lambda shape: pl.BlockSpec(shape, lambda b: (0,) * len(shape))
    m3, r_new = pl.pallas_call(
        functools.partial(_mixer_sample_kernel, pos0=pos0),
        grid=(batch // bb,),
        in_specs=[
            pl.BlockSpec((bb, seq, d_pool), lambda b: (b, 0, 0)),
            pl.BlockSpec((bb, POOL_MAX - 1, d_pool), lambda b: (b, 0, 0)),
            pl.BlockSpec((bb, seq, d_qk), lambda b: (b, 0, 0)),
            pl.BlockSpec((bb, seq, d_qk), lambda b: (b, 0, 1)),
            pl.BlockSpec((bb, seq, d_v), lambda b: (b, 0, 1)),
            pl.BlockSpec((bb, seq, d_v), lambda b: (b, 0, 2)),
            pl.BlockSpec((bb, seq, d_v), lambda b: (b, 0, 3)),
            pl.BlockSpec((bb, seq, d_v), lambda b: (b, 0, 4)),
            pl.BlockSpec((bb, N_HEADS, DK, DV), lambda b: (b, 0, 0, 0)),
            full(w_pool_b.shape), full(pool_scale.shape), full(gn_gain.shape),
            full(dmask.shape), full(xi_b.shape), full(zeta_b.shape), full(gc_b.shape),
        ],
        out_specs=[
            pl.BlockSpec((bb, seq, d_v), lambda b: (b, 0, 0)),
            pl.BlockSpec((bb, N_HEADS, DK, DV), lambda b: (b, 0, 0, 0)),
        ],
        out_shape=[
            jax.ShapeDtypeStruct((batch, seq, d_v), F32),
            jax.ShapeDtypeStruct((batch, N_HEADS, DK, DV), F32),
        ],
        scratch_shapes=[pltpu.VMEM((bb, POOL_MAX + seq, d_pool), F32)],
        compiler_params=_params("arbitrary"),
        name="mixer_sample",
    )(u3, state_pool, rest3, rest3, rest3, rest3, rest3, rest3, state_ret,
      w_pool_b, pool_scale, gn_gain, dmask, xi_b, zeta_b, gc_b)
    return m3.reshape(batch * seq, d_v), r_new


def _out_proj_kernel(m_ref, w_ref, x_ref, gpost_ref, gpre_ref, x1_ref, h2_ref):
    proj = jnp.dot(m_ref[...].astype(BF16), w_ref[...], preferred_element_type=F32)
    x1 = x_ref[...] + _rms(proj, gpost_ref[...])
    x1_ref[...] = x1
    h2_ref[...] = _rms(x1, gpre_ref[...]).astype(BF16)


def _out_proj(m2, w_out_b, x2, g_post, g_pre):
    m, d = x2.shape
    tm = OUT_ROW_TILE
    assert m % tm == 0
    return pl.pallas_call(
        _out_proj_kernel,
        grid=(m // tm,),
        in_specs=[
            pl.BlockSpec((tm, m2.shape[1]), lambda i: (i, 0)),
            pl.BlockSpec(w_out_b.shape, lambda i: (0, 0)),
            pl.BlockSpec((tm, d), lambda i: (i, 0)),
            pl.BlockSpec((1, d), lambda i: (0, 0)),
            pl.BlockSpec((1, d), lambda i: (0, 0)),
        ],
        out_specs=[
            pl.BlockSpec((tm, d), lambda i: (i, 0)),
            pl.BlockSpec((tm, d), lambda i: (i, 0)),
        ],
        out_shape=[jax.ShapeDtypeStruct((m, d), F32), jax.ShapeDtypeStruct((m, d), BF16)],
        compiler_params=_params("arbitrary"),
        name="out_proj",
    )(m2, w_out_b, x2, g_post, g_pre)


def _gelu_tanh(x):
    return 0.5 * x * (1.0 + jnp.tanh(0.7978845608028654 * (x + 0.044715 * (x * x * x))))


def _conv_ffn_kernel(*refs, seq, blocks_per_seq):
    if seq >= FF_ROW_TILE:
        (h2_ref, wv_ref, wg_ref, cwv_ref, cwg_ref, cbv_ref, cbg_ref, wd_ref, x1_ref, g_ref,
         y_ref, tail_ref, act_cur, act_prev, carry_scr) = refs
    else:
        (h2_ref, wv_ref, wg_ref, cwv_ref, cwg_ref, cbv_ref, cbg_ref, wd_ref, x1_ref, g_ref, stv_ref, stg_ref,
         y_ref, tail_ref, act_cur, act_prev) = refs
    i = pl.program_id(0)
    j = pl.program_id(1)
    nj = pl.num_programs(1) - 1
    tm = h2_ref.shape[0]
    hist = CONV_K - 1

    if seq >= FF_ROW_TILE:
        @pl.when((i == 0) & (j == 0))
        def _():
            carry_scr[...] = jnp.zeros(carry_scr.shape, F32)

    def conv_long(part, up, cw_ref, cb_ref):
        prev = jnp.where(i % blocks_per_seq == 0, 0.0, carry_scr[j, part])
        row8 = lax.broadcasted_iota(jnp.int32, prev.shape, 0)
        out = up * cw_ref[hist:hist + 1, :] + cb_ref[...]
        for lag in range(1, CONV_K):
            rolled = pltpu.roll(up, lag, axis=0)
            top = jnp.where(row8 < lag, pltpu.roll(prev, lag, axis=0), rolled[0:8])
            shifted = jnp.concatenate([top, rolled[8:]], axis=0)
            out = out + shifted * cw_ref[hist - lag:hist - lag + 1, :]
        last = up[tm - 8:tm]
        carry_scr[j, part] = last
        tail_ref[0, part] = last
        return out

    def conv_short(part, up, cw_ref, cb_ref, st_ref):
        ns = tm // seq
        up3 = up.reshape(ns, seq, up.shape[1])
        t = lax.broadcasted_iota(jnp.int32, up3.shape, 1)
        out = up3 * cw_ref[hist:hist + 1, :] + cb_ref[...]
        for lag in range(1, CONV_K):
            shifted = pltpu.roll(up3, lag, axis=1)
            for r in range(lag):
                row = st_ref[:, hist - lag + r:hist - lag + r + 1, :]
                shifted = jnp.where(t == r, row, shifted)
            out = out + shifted * cw_ref[hist - lag:hist - lag + 1, :]
        tail_ref[:, part] = up3[:, seq - hist:, :]
        return out.reshape(tm, up.shape[1])

    def up_phase():
        h2 = h2_ref[...]
        upv = jnp.dot(h2, wv_ref[...], preferred_element_type=F32)
        upg = jnp.dot(h2, wg_ref[...], preferred_element_type=F32)
        if seq >= FF_ROW_TILE:
            val = conv_long(0, upv, cwv_ref, cbv_ref)
            gate = conv_long(1, upg, cwg_ref, cbg_ref)
        else:
            val = conv_short(0, upv, cwv_ref, cbv_ref, stv_ref)
            gate = conv_short(1, upg, cwg_ref, cbg_ref, stg_ref)
        act_cur[...] = (_gelu_tanh(gate) * val).astype(BF16)

    @pl.when(j == 0)
    def _():
        y_ref[...] = jnp.zeros(y_ref.shape, F32)
        up_phase()

    @pl.when((j > 0) & (j < nj))
    def _():
        act_prev[...] = act_cur[...]
        up_phase()
        y_ref[...] += jnp.dot(act_prev[...], wd_ref[...], preferred_element_type=F32)

    @pl.when(j == nj)
    def _():
        acc = y_ref[...] + jnp.dot(act_cur[...], wd_ref[...], preferred_element_type=F32)
        y_ref[...] = x1_ref[...] + _rms(acc, g_ref[...])


def _conv_ffn(h2, x1, w_up_b, conv_w, conv_b, w_down_b, g_post, state_conv, *, seq):
    m, d = x1.shape
    d_ff = w_down_b.shape[0]
    tm, tf = FF_ROW_TILE, FF_COL_TILE
    assert m % tm == 0 and d_ff % tf == 0
    nj = d_ff // tf
    hist = CONV_K - 1
    conv_b2 = conv_b.reshape(1, 2 * d_ff)
    long_seq = seq >= tm
    up_j = lambda j: jnp.minimum(j, nj - 1)
    down_j = lambda j: jnp.maximum(j - 1, 0)
    in_specs = [
        pl.BlockSpec((tm, d), lambda i, j: (i, 0)),
        pl.BlockSpec((d, tf), lambda i, j: (0, up_j(j))),
        pl.BlockSpec((d, tf), lambda i, j: (0, up_j(j) + nj)),
        pl.BlockSpec((CONV_K, tf), lambda i, j: (0, up_j(j))),
        pl.BlockSpec((CONV_K, tf), lambda i, j: (0, up_j(j) + nj)),
        pl.BlockSpec((1, tf), lambda i, j: (0, up_j(j))),
        pl.BlockSpec((1, tf), lambda i, j: (0, up_j(j) + nj)),
        pl.BlockSpec((tf, d), lambda i, j: (down_j(j), 0)),
        pl.BlockSpec((tm, d), lambda i, j: (i, 0), pipeline_mode=pl.Buffered(1)),
        pl.BlockSpec((1, d), lambda i, j: (0, 0)),
    ]
    args = [h2, w_up_b, w_up_b, conv_w, conv_w, conv_b2, conv_b2, w_down_b, x1, g_post]
    scratch = [pltpu.VMEM((tm, tf), BF16), pltpu.VMEM((tm, tf), BF16)]
    if long_seq:
        assert seq % tm == 0
        blocks_per_seq = seq // tm
        tail_shape = (m // tm, 2, 8, d_ff)
        tail_spec = pl.BlockSpec((1, 2, 8, tf), lambda i, j: (i, 0, 0, up_j(j)))
        scratch += [pltpu.VMEM((nj, 2, 8, tf), F32)]
    else:
        assert tm % seq == 0 and seq == 8 and seq >= hist
        blocks_per_seq = 0
        ns = tm // seq
        in_specs += [
            pl.BlockSpec((ns, hist, tf), lambda i, j: (i, 0, up_j(j))),
            pl.BlockSpec((ns, hist, tf), lambda i, j: (i, 0, up_j(j) + nj)),
        ]
        args += [state_conv, state_conv]
        tail_shape = (m // seq, 2, hist, d_ff)
        tail_spec = pl.BlockSpec((ns, 2, hist, tf), lambda i, j: (i, 0, 0, up_j(j)))
    y, tail = pl.pallas_call(
        functools.partial(_conv_ffn_kernel, seq=seq, blocks_per_seq=blocks_per_seq),
        grid=(m // tm, nj + 1),
        in_specs=in_specs,
        out_specs=[pl.BlockSpec((tm, d), lambda i, j: (i, 0)), tail_spec],
        out_shape=[jax.ShapeDtypeStruct((m, d), F32), jax.ShapeDtypeStruct(tail_shape, F32)],
        scratch_shapes=scratch,
        compiler_params=_params("arbitrary", "arbitrary"),
        name="conv_ffn",
    )(*args)
    if long_seq:
        tail = tail[blocks_per_seq - 1::blocks_per_seq, :, 8 - hist:, :]
    nseq = tail.shape[0]
    new_conv = jnp.swapaxes(tail, 1, 2).reshape(nseq, hist, 2 * d_ff)
    return y, new_conv


def _rotary_table(pos, rows):
    half = DK // 2
    theta = ROPE_BASE ** (-jnp.arange(half, dtype=F32) / half)
    ang = pos.astype(F32)[:, None] * theta[None, :]
    cos, sin = jnp.cos(ang), jnp.sin(ang)
    cs = jnp.concatenate([cos, cos, -sin, sin], axis=-1)
    reps = max(1, rows // cs.shape[0])
    return jnp.tile(cs, (reps, 1))


def kernel(x_prompt, x_sample, state_pool, state_ret, state_conv, g_pre_mix, w_in, w_pool, pool_scale, gn_gain,
           w_out, g_post_mix, g_pre_ffn, w_up, conv_w, conv_b, w_down, g_post_ffn):
    d = x_prompt.shape[-1]
    d_pool = state_pool.shape[-1]
    row = lambda v: v.reshape(1, -1)
    w_in_b, w_pool_b = w_in.astype(BF16), w_pool.astype(BF16)

    bp, lp, _ = x_prompt.shape
    xp = x_prompt.reshape(bp * lp, d)
    cs_p = _rotary_table(jnp.arange(lp, dtype=jnp.int32), IN_ROW_TILE)
    u_p, rest_p, w_up_rows_b, w_down_b, w_out_b = _in_proj(
        xp, row(g_pre_mix), cs_p, w_in_b, d_pool=d_pool, rest_dtype=BF16,
        cast=(w_up.reshape(w_up.shape[1], d), w_down, w_out))
    w_up_b = w_up_rows_b.reshape(w_up.shape)

    def dense_tail(x2, m2, state_conv_path, seq):
        x1, h2 = _out_proj(m2, w_out_b, x2, row(g_post_mix), row(g_pre_ffn))
        return _conv_ffn(h2, x1, w_up_b, conv_w, conv_b, w_down_b, row(g_post_ffn), state_conv_path, seq=seq)

    m_p, ret_p = _mixer_prompt(u_p, rest_p, w_pool_b, row(pool_scale), row(gn_gain), batch=bp, seq=lp)
    y_p, conv_p = dense_tail(xp, m_p, None, lp)
    pool_p = u_p.reshape(bp, lp, d_pool)[:, lp - (POOL_MAX - 1):]

    bs, ls, _ = x_sample.shape
    xs = x_sample.reshape(bs * ls, d)
    cs_s = _rotary_table(PAST_LEN + jnp.arange(ls, dtype=jnp.int32), IN_ROW_TILE)
    u_s, rest_s = _in_proj(xs, row(g_pre_mix), cs_s, w_in_b, d_pool=d_pool, rest_dtype=F32)
    m_s, ret_s = _mixer_sample(u_s, rest_s, state_pool, state_ret, w_pool_b, row(pool_scale), row(gn_gain),
                               batch=bs, seq=ls, pos0=PAST_LEN)
    y_s, conv_s = dense_tail(xs, m_s, state_conv, ls)
    pool_s = jnp.concatenate([state_pool, u_s.reshape(bs, ls, d_pool)], axis=1)[:, -(POOL_MAX - 1):]

    return (y_p.reshape(bp, lp, d), y_s.reshape(bs, ls, d), pool_p, ret_p.astype(x_prompt.dtype), conv_p,
            pool_s, ret_s.astype(x_sample.dtype), conv_s)
```

```python
import functools

import jax
import jax.numpy as jnp
from jax import lax
from jax.experimental import pallas as pl
from jax.experimental.pallas import tpu as pltpu

F32 = jnp.float32
BF16 = jnp.bfloat16

EPS = 1e-6
N_HEADS = 8
DK = 128
DV = 256
POOL_WINDOWS = (2, 4, 8, 16)
POOL_MAX = 16
CONV_K = 3
ROPE_BASE = 10000.0
PAST_LEN = 16384
RET_CHUNK = 128

VMEM_LIMIT_BYTES = 56 * 1024 * 1024
OUT_ROW_TILE = 512
IN_ROW_TILE = 1024
FF_ROW_TILE = 1024
IN_COL_TILE = 1024
FF_COL_TILE = 512
MIX_ROWS = 512
MIX_SEQS = 8
BF16_ROWS = 16


def _params(*sem):
    return pltpu.CompilerParams(dimension_semantics=sem, vmem_limit_bytes=VMEM_LIMIT_BYTES)


def _rms(x, g):
    return x * lax.rsqrt(jnp.mean(x * x, axis=-1, keepdims=True) + EPS) * g


def _in_proj_kernel(*refs, kinds, n_cast):
    x_ref, g_ref, cs_ref, w_ref = refs[:4]
    cast_src = refs[4:4 + n_cast]
    u_ref, p_ref = refs[4 + n_cast:6 + n_cast]
    cast_dst = refs[6 + n_cast:6 + 2 * n_cast]
    h_scr = refs[6 + 2 * n_cast]
    j = pl.program_id(1)

    @pl.when(j == 0)
    def _():
        h_scr[...] = _rms(x_ref[...], g_ref[...]).astype(BF16)

    def proj():
        for src, dst in zip(cast_src, cast_dst):
            dst[...] = src[...].astype(BF16)
        return jnp.dot(h_scr[...], w_ref[...], preferred_element_type=F32)

    def rotary(acc, scale):
        cos = cs_ref[:, :DK]
        sin = cs_ref[:, DK:]
        for h in range(acc.shape[1] // DK):
            xh = acc[:, h * DK:(h + 1) * DK]
            rot = xh * cos + pltpu.roll(xh, DK // 2, axis=1) * sin
            if scale != 1.0:
                rot = rot * scale
            p_ref[:, h * DK:(h + 1) * DK] = rot.astype(p_ref.dtype)

    def tiles_of(kind):
        return [t for t, k in enumerate(kinds) if k == kind]

    def cond_of(kind):
        ts = tiles_of(kind)
        return (j >= ts[0]) & (j <= ts[-1])

    @pl.when(cond_of("u"))
    def _():
        u_ref[...] = proj()

    @pl.when(cond_of("q"))
    def _():
        rotary(proj(), 1.0)

    @pl.when(cond_of("k"))
    def _():
        rotary(proj(), DK ** -0.5)

    @pl.when(cond_of("v"))
    def _():
        p_ref[...] = proj().astype(p_ref.dtype)

    def sigmoid(x):
        return 0.5 * jnp.tanh(0.5 * x) + 0.5

    @pl.when(cond_of("silu"))
    def _():
        acc = proj()
        p_ref[...] = (acc * sigmoid(acc)).astype(p_ref.dtype)

    @pl.when(cond_of("sig"))
    def _():
        p_ref[...] = sigmoid(proj()).astype(p_ref.dtype)


def _in_proj(x2, g, cs, w_in_b, *, d_pool, rest_dtype, cast=()):
    m, d = x2.shape
    n_in = w_in_b.shape[1]
    tm, tn = IN_ROW_TILE, IN_COL_TILE
    assert m % tm == 0 and n_in % tn == 0 and d_pool == tn and cs.shape[0] % tm == 0
    d_qk, d_v = N_HEADS * DK, N_HEADS * DV
    kinds = (["u"] * (d_pool // tn) + ["q"] * (d_qk // tn) + ["k"] * (d_qk // tn) + ["v"] * (d_v // tn)
             + ["silu"] * (d_v // tn) + ["sig"] * (2 * d // tn))
    nj = n_in // tn
    assert len(kinds) == nj
    cs_blocks = cs.shape[0] // tm
    steps = (m // tm) * nj
    cast_specs, cast_shapes = [], []
    for a in cast:
        rows, cols = a.shape
        rb = -(-rows // (steps * BF16_ROWS)) * BF16_ROWS
        while rows % rb:
            rb += BF16_ROWS
        nblk = rows // rb
        cast_specs.append(pl.BlockSpec((rb, cols), lambda i, j, nblk=nblk: (jnp.minimum(i * nj + j, nblk - 1), 0)))
        cast_shapes.append(jax.ShapeDtypeStruct((rows, cols), BF16))
    return pl.pallas_call(
        functools.partial(_in_proj_kernel, kinds=tuple(kinds), n_cast=len(cast)),
        grid=(m // tm, nj),
        in_specs=[
            pl.BlockSpec((tm, d), lambda i, j: (i, 0)),
            pl.BlockSpec((1, d), lambda i, j: (0, 0)),
            pl.BlockSpec((tm, 2 * DK), lambda i, j: (i % cs_blocks, 0)),
            pl.BlockSpec((d, tn), lambda i, j: (0, j)),
        ] + cast_specs,
        out_specs=[
            pl.BlockSpec((tm, d_pool), lambda i, j: (i, 0)),
            pl.BlockSpec((tm, tn), lambda i, j: (i, jnp.maximum(j - d_pool // tn, 0))),
        ] + cast_specs,
        out_shape=[
            jax.ShapeDtypeStruct((m, d_pool), F32),
            jax.ShapeDtypeStruct((m, n_in - d_pool), rest_dtype),
        ] + cast_shapes,
        scratch_shapes=[pltpu.VMEM((tm, d), BF16)],
        compiler_params=_params("arbitrary", "arbitrary"),
        name="in_proj",
    )(x2, g, cs, w_in_b, *cast)


def _group_norm(o):
    mu = jnp.mean(o, axis=-1, keepdims=True)
    d = o - mu
    var = jnp.mean(d * d, axis=-1, keepdims=True)
    return d * lax.rsqrt(var + EPS)


def _decay_tables(c, xi_width):
    log_g = jnp.log(1.0 - 2.0 ** (-5.0 - jnp.arange(N_HEADS, dtype=F32)))
    idx = jnp.arange(c, dtype=F32)
    rel = idx[:, None] - idx[None, :]
    dmask = jnp.where(rel >= 0, jnp.exp(jnp.maximum(rel, 0.0)[None] * log_g[:, None, None]), 0.0)
    xi = jnp.exp((idx + 1.0)[None, :] * log_g[:, None])
    zeta = jnp.exp((c - 1.0 - idx)[None, :] * log_g[:, None])
    g_c = jnp.exp(c * log_g)
    xi_b = jnp.broadcast_to(xi[:, :, None], (N_HEADS, c, xi_width))
    zeta_b = jnp.broadcast_to(zeta[:, :, None], (N_HEADS, c, DK))
    gc_b = jnp.broadcast_to(g_c[:, None, None], (N_HEADS, 1, DV))
    return dmask, xi_b, zeta_b, gc_b


def _mixer_prompt_kernel(u_ref, q_ref, k_ref, v_ref, sr_ref, sa_ref, sg_ref,
                         wp_ref, ps_ref, gn_ref, dm_ref, xi_ref, zt_ref, gc_ref,
                         m_ref, rout_ref, ext_scr, r_scr):
    c = pl.program_id(1)
    rb = u_ref.shape[0]
    halo = POOL_MAX

    @pl.when(c == 0)
    def _():
        ext_scr[0:halo, :] = jnp.zeros((halo, ext_scr.shape[1]), F32)
        r_scr[...] = jnp.zeros(r_scr.shape, F32)

    ext_scr[halo:halo + rb, :] = u_ref[...]

    pg = u_ref.shape[1] // len(POOL_WINDOWS)
    pos = c * rb + lax.broadcasted_iota(jnp.int32, (rb, pg), 0)
    a_groups = []
    for g, w in enumerate(POOL_WINDOWS):
        cols = slice(g * pg, (g + 1) * pg)
        ext = ext_scr[:, cols]
        wsum = ext
        span = 1
        while span < w:
            wsum = wsum + pltpu.roll(wsum, span, axis=0)
            span *= 2
        cnt = jnp.minimum(w, pos + 1).astype(F32)
        z = wsum[halo:] / cnt - ext[halo:]
        a_groups.append(jnp.dot(z.astype(BF16), wp_ref[g], preferred_element_type=F32))
    og = a_groups[0].shape[1]

    ext_scr[0:halo, :] = ext_scr[rb:rb + halo, :]

    for r0 in range(0, rb, RET_CHUNK):
        rows = slice(r0, r0 + RET_CHUNK)
        for h in range(N_HEADS):
            qh = q_ref[rows, h * DK:(h + 1) * DK]
            kh = k_ref[rows, h * DK:(h + 1) * DK]
            vh = v_ref[rows, h * DV:(h + 1) * DV]
            r_old = r_scr[h]
            s = lax.dot_general(qh, kh, (((1,), (1,)), ((), ())), preferred_element_type=F32) * dm_ref[h]
            inter = jnp.dot(qh, r_old.astype(BF16), preferred_element_type=F32) * xi_ref[h]
            o = jnp.dot(s.astype(BF16), vh, preferred_element_type=F32) + inter
            kz = (kh.astype(F32) * zt_ref[h]).astype(BF16)
            r_scr[h] = r_old * gc_ref[h] + lax.dot_general(
                kz, vh, (((0,), (0,)), ((), ())), preferred_element_type=F32)

            hc = slice(h * DV, (h + 1) * DV)
            on = _group_norm(o) * gn_ref[:, hc]
            r = sr_ref[rows, hc].astype(F32) * on
            g, part = divmod(h * DV, og)
            a = a_groups[g][rows, part:part + DV] * ps_ref[:, hc]
            mh = sa_ref[rows, hc].astype(F32) * a + sg_ref[rows, hc].astype(F32) * r
            m_ref[rows, hc] = mh.astype(m_ref.dtype)

    @pl.when(c == pl.num_programs(1) - 1)
    def _():
        rout_ref[0] = r_scr[...]


def _mixer_prompt(u, rest, w_pool_b, pool_scale, gn_gain, *, batch, seq):
    m_rows, d_pool = u.shape
    d_v = N_HEADS * DV
    rb = MIX_ROWS
    assert seq % rb == 0 and rb % RET_CHUNK == 0 and m_rows == batch * seq
    nc = seq // rb
    dmask, xi_b, zeta_b, gc_b = _decay_tables(RET_CHUNK, DV)
    d_qk = N_HEADS * DK
    row = lambda b, c: b * nc + c
    full = lambda shape: pl.BlockSpec(shape, lambda b, c: (0,) * len(shape))
    return pl.pallas_call(
        _mixer_prompt_kernel,
        grid=(batch, nc),
        in_specs=[
            pl.BlockSpec((rb, d_pool), lambda b, c: (row(b, c), 0)),
            pl.BlockSpec((rb, d_qk), lambda b, c: (row(b, c), 0)),
            pl.BlockSpec((rb, d_qk), lambda b, c: (row(b, c), 1)),
            pl.BlockSpec((rb, d_v), lambda b, c: (row(b, c), 1)),
            pl.BlockSpec((rb, d_v), lambda b, c: (row(b, c), 2)),
            pl.BlockSpec((rb, d_v), lambda b, c: (row(b, c), 3)),
            pl.BlockSpec((rb, d_v), lambda b, c: (row(b, c), 4)),
            full(w_pool_b.shape), full(pool_scale.shape), full(gn_gain.shape),
            full(dmask.shape), full(xi_b.shape), full(zeta_b.shape), full(gc_b.shape),
        ],
        out_specs=[
            pl.BlockSpec((rb, d_v), lambda b, c: (row(b, c), 0)),
            pl.BlockSpec((1, N_HEADS, DK, DV), lambda b, c: (b, 0, 0, 0)),
        ],
        out_shape=[
            jax.ShapeDtypeStruct((m_rows, d_v), BF16),
            jax.ShapeDtypeStruct((batch, N_HEADS, DK, DV), F32),
        ],
        scratch_shapes=[pltpu.VMEM((rb + POOL_MAX, d_pool), F32), pltpu.VMEM((N_HEADS, DK, DV), F32)],
        compiler_params=_params("arbitrary", "arbitrary"),
        name="mixer_prompt",
    )(u, rest, rest, rest, rest, rest, rest, w_pool_b, pool_scale, gn_gain, dmask, xi_b, zeta_b, gc_b)


def _mixer_sample_kernel(u_ref, sp_ref, q_ref, k_ref, v_ref, sr_ref, sa_ref, sg_ref, rin_ref,
                         wp_ref, ps_ref, gn_ref, dm_ref, xi_ref, zt_ref, gc_ref,
                         m_ref, rout_ref, ext_scr, *, pos0):
    bb, ln, d_pool = u_ref.shape
    hist = POOL_MAX - 1
    ext_scr[:, 1:1 + hist, :] = sp_ref[...]
    ext_scr[:, POOL_MAX:POOL_MAX + ln, :] = u_ref[...]

    pg = d_pool // len(POOL_WINDOWS)
    pos = pos0 + lax.broadcasted_iota(jnp.int32, (bb, ln, pg), 1)
    a_groups = []
    for g, w in enumerate(POOL_WINDOWS):
        cols = slice(g * pg, (g + 1) * pg)
        wsum = ext_scr[:, POOL_MAX:POOL_MAX + ln, cols]
        for lag in range(1, w):
            wsum = wsum + ext_scr[:, POOL_MAX - lag:POOL_MAX - lag + ln, cols]
        cnt = jnp.minimum(w, pos + 1).astype(F32)
        z = wsum / cnt - ext_scr[:, POOL_MAX:POOL_MAX + ln, cols]
        ag = jnp.dot(z.reshape(bb * ln, pg).astype(BF16), wp_ref[g], preferred_element_type=F32)
        a_groups.append(ag.reshape(bb, ln, ag.shape[-1]))
    og = a_groups[0].shape[-1]

    for h in range(N_HEADS):
        qh = q_ref[:, :, h * DK:(h + 1) * DK].astype(BF16)
        kf = k_ref[:, :, h * DK:(h + 1) * DK].astype(F32)
        kh = kf.astype(BF16)
        vh = v_ref[:, :, h * DV:(h + 1) * DV].astype(BF16)
        r_old = rin_ref[:, h]
        s = jnp.einsum("btd,bsd->bts", qh, kh, preferred_element_type=F32) * dm_ref[h]
        inter = jnp.einsum("btd,bdv->btv", qh, r_old.astype(BF16), preferred_element_type=F32) * xi_ref[h]
        o = jnp.einsum("bts,bsv->btv", s.astype(BF16), vh, preferred_element_type=F32) + inter
        kz_t = jnp.swapaxes(kf * zt_ref[h], 1, 2).astype(BF16)
        rout_ref[:, h] = r_old * gc_ref[h] + jnp.einsum(
            "bds,bsv->bdv", kz_t, vh, preferred_element_type=F32)

        hc = slice(h * DV, (h + 1) * DV)
        on = _group_norm(o) * gn_ref[:, hc]
        r = sr_ref[:, :, hc].astype(F32) * on
        g, part = divmod(h * DV, og)
        a = a_groups[g][:, :, part:part + DV] * ps_ref[:, hc]
        mh = sa_ref[:, :, hc].astype(F32) * a + sg_ref[:, :, hc].astype(F32) * r
        m_ref[:, :, hc] = mh.astype(m_ref.dtype)


def _mixer_sample(u, rest, state_pool, state_ret, w_pool_b, pool_scale, gn_gain, *, batch, seq, pos0):
    d_pool = u.shape[-1]
    d_v, d_qk = N_HEADS * DV, N_HEADS * DK
    bb = MIX_SEQS
    assert batch % bb == 0
    dmask, xi_b, zeta_b, gc_b = _decay_tables(seq, DV)
    u3 = u.reshape(batch, seq, d_pool)
    rest3 = rest.reshape(batch, seq, rest.shape[-1])
    full = lambda shape: pl.BlockSpec(shape, lambda b: (0,) * len(shape))
    m3, r_new = pl.pallas_call(
        functools.partial(_mixer_sample_kernel, pos0=pos0),
        grid=(batch // bb,),
        in_specs=[
            pl.BlockSpec((bb, seq, d_pool), lambda b: (b, 0, 0)),
            pl.BlockSpec((bb, POOL_MAX - 1, d_pool), lambda b: (b, 0, 0)),
            pl.BlockSpec((bb, seq, d_qk), lambda b: (b, 0, 0)),
            pl.BlockSpec((bb, seq, d_qk), lambda b: (b, 0, 1)),
            pl.BlockSpec((bb, seq, d_v), lambda b: (b, 0, 1)),
            pl.BlockSpec((bb, seq, d_v), lambda b: (b, 0, 2)),
            pl.BlockSpec((bb, seq, d_v), lambda b: (b, 0, 3)),
            pl.BlockSpec((bb, seq, d_v), lambda b: (b, 0, 4)),
            pl.BlockSpec((bb, N_HEADS, DK, DV), lambda b: (b, 0, 0, 0)),
            full(w_pool_b.shape), full(pool_scale.shape), full(gn_gain.shape),
            full(dmask.shape), full(xi_b.shape), full(zeta_b.shape), full(gc_b.shape),
        ],
        out_specs=[
            pl.BlockSpec((bb, seq, d_v), lambda b: (b, 0, 0)),
            pl.BlockSpec((bb, N_HEADS, DK, DV), lambda b: (b, 0, 0, 0)),
        ],
        out_shape=[
            jax.ShapeDtypeStruct((batch, seq, d_v), F32),
            jax.ShapeDtypeStruct((batch, N_HEADS, DK, DV), F32),
        ],
        scratch_shapes=[pltpu.VMEM((bb, POOL_MAX + seq, d_pool), F32)],
        compiler_params=_params("arbitrary"),
        name="mixer_sample",
    )(u3, state_pool, rest3, rest3, rest3, rest3, rest3, rest3, state_ret,
      w_pool_b, pool_scale, gn_gain, dmask, xi_b, zeta_b, gc_b)
    return m3.reshape(batch * seq, d_v), r_new


def _out_proj_kernel(m_ref, w_ref, x_ref, gpost_ref, gpre_ref, x1_ref, h2_ref):
    proj = jnp.dot(m_ref[...].astype(BF16), w_ref[...], preferred_element_type=F32)
    x1 = x_ref[...] + _rms(proj, gpost_ref[...])
    x1_ref[...] = x1
    h2_ref[...] = _rms(x1, gpre_ref[...]).astype(BF16)


def _out_proj(m2, w_out_b, x2, g_post, g_pre):
    m, d = x2.shape
    tm = OUT_ROW_TILE
    assert m % tm == 0
    return pl.pallas_call(
        _out_proj_kernel,
        grid=(m // tm,),
        in_specs=[
            pl.BlockSpec((tm, m2.shape[1]), lambda i: (i, 0)),
            pl.BlockSpec(w_out_b.shape, lambda i: (0, 0)),
            pl.BlockSpec((tm, d), lambda i: (i, 0)),
            pl.BlockSpec((1, d), lambda i: (0, 0)),
            pl.BlockSpec((1, d), lambda i: (0, 0)),
        ],
        out_specs=[
            pl.BlockSpec((tm, d), lambda i: (i, 0)),
            pl.BlockSpec((tm, d), lambda i: (i, 0)),
        ],
        out_shape=[jax.ShapeDtypeStruct((m, d), F32), jax.ShapeDtypeStruct((m, d), BF16)],
        compiler_params=_params("arbitrary"),
        name="out_proj",
    )(m2, w_out_b, x2, g_post, g_pre)


def _gelu_tanh(x):
    return 0.5 * x * (1.0 + jnp.tanh(0.7978845608028654 * (x + 0.044715 * (x * x * x))))


def _conv_ffn_kernel(*refs, seq, blocks_per_seq):
    if seq >= FF_ROW_TILE:
        (h2_ref, wv_ref, wg_ref, cwv_ref, cwg_ref, cbv_ref, cbg_ref, wd_ref, x1_ref, g_ref,
         y_ref, tail_ref, act_cur, act_prev, carry_scr) = refs
    else:
        (h2_ref, wv_ref, wg_ref, cwv_ref, cwg_ref, cbv_ref, cbg_ref, wd_ref, x1_ref, g_ref, stv_ref, stg_ref,
         y_ref, tail_ref, act_cur, act_prev) = refs
    i = pl.program_id(0)
    j = pl.program_id(1)
    nj = pl.num_programs(1) - 1
    tm = h2_ref.shape[0]
    hist = CONV_K - 1

    if seq >= FF_ROW_TILE:
        @pl.when((i == 0) & (j == 0))
        def _():
            carry_scr[...] = jnp.zeros(carry_scr.shape, F32)

    def conv_long(part, up, cw_ref, cb_ref):
        prev = jnp.where(i % blocks_per_seq == 0, 0.0, carry_scr[j, part])
        row8 = lax.broadcasted_iota(jnp.int32, prev.shape, 0)
        out = up * cw_ref[hist:hist + 1, :] + cb_ref[...]
        for lag in range(1, CONV_K):
            rolled = pltpu.roll(up, lag, axis=0)
            top = jnp.where(row8 < lag, pltpu.roll(prev, lag, axis=0), rolled[0:8])
            shifted = jnp.concatenate([top, rolled[8:]], axis=0)
            out = out + shifted * cw_ref[hist - lag:hist - lag + 1, :]
        last = up[tm - 8:tm]
        carry_scr[j, part] = last
        tail_ref[0, part] = last
        return out

    def conv_short(part, up, cw_ref, cb_ref, st_ref):
        ns = tm // seq
        up3 = up.reshape(ns, seq, up.shape[1])
        t = lax.broadcasted_iota(jnp.int32, up3.shape, 1)
        out = up3 * cw_ref[hist:hist + 1, :] + cb_ref[...]
        for lag in range(1, CONV_K):
            shifted = pltpu.roll(up3, lag, axis=1)
            for r in range(lag):
                row = st_ref[:, hist - lag + r:hist - lag + r + 1, :]
                shifted = jnp.where(t == r, row, shifted)
            out = out + shifted * cw_ref[hist - lag:hist - lag + 1, :]
        tail_ref[:, part] = up3[:, seq - hist:, :]
        return out.reshape(tm, up.shape[1])

    def up_phase():
        h2 = h2_ref[...]
        upv = jnp.dot(h2, wv_ref[...], preferred_element_type=F32)
        upg = jnp.dot(h2, wg_ref[...], preferred_element_type=F32)
        if seq >= FF_ROW_TILE:
            val = conv_long(0, upv, cwv_ref, cbv_ref)
            gate = conv_long(1, upg, cwg_ref, cbg_ref)
        else:
            val = conv_short(0, upv, cwv_ref, cbv_ref, stv_ref)
            gate = conv_short(1, upg, cwg_ref, cbg_ref, stg_ref)
        act_cur[...] = (_gelu_tanh(gate) * val).astype(BF16)

    @pl.when(j == 0)
    def _():
        y_ref[...] = jnp.zeros(y_ref.shape, F32)
        up_phase()

    @pl.when((j > 0) & (j < nj))
    def _():
        act_prev[...] = act_cur[...]
        up_phase()
        y_ref[...] += jnp.dot(act_prev[...], wd_ref[...], preferred_element_type=F32)

    @pl.when(j == nj)
    def _():
        acc = y_ref[...] + jnp.dot(act_cur[...], wd_ref[...], preferred_element_type=F32)
        y_ref[...] = x1_ref[...] + _rms(acc, g_ref[...])


def _conv_ffn(h2, x1, w_up_b, conv_w, conv_b, w_down_b, g_post, state_conv, *, seq):
    m, d = x1.shape
    d_ff = w_down_b.shape[0]
    tm, tf = FF_ROW_TILE, FF_COL_TILE
    assert m % tm == 0 and d_ff % tf == 0
    nj = d_ff // tf
    hist = CONV_K - 1
    conv_b2 = conv_b.reshape(1, 2 * d_ff)
    long_seq = seq >= tm
    up_j = lambda j: jnp.minimum(j, nj - 1)
    down_j = lambda j: jnp.maximum(j - 1, 0)
    in_specs = [
        pl.BlockSpec((tm, d), lambda i, j: (i, 0)),
        pl.BlockSpec((d, tf), lambda i, j: (0, up_j(j))),
        pl.BlockSpec((d, tf), lambda i, j: (0, up_j(j) + nj)),
        pl.BlockSpec((CONV_K, tf), lambda i, j: (0, up_j(j))),
        pl.BlockSpec((CONV_K, tf), lambda i, j: (0, up_j(j) + nj)),
        pl.BlockSpec((1, tf), lambda i, j: (0, up_j(j))),
        pl.BlockSpec((1, tf), lambda i, j: (0, up_j(j) + nj)),
        pl.BlockSpec((tf, d), lambda i, j: (down_j(j), 0)),
        pl.BlockSpec((tm, d), lambda i, j: (i, 0), pipeline_mode=pl.Buffered(1)),
        pl.BlockSpec((1, d), lambda i, j: (0, 0)),
    ]
    args = [h2, w_up_b, w_up_b, conv_w, conv_w, conv_b2, conv_b2, w_down_b, x1, g_post]
    scratch = [pltpu.VMEM((tm, tf), BF16), pltpu.VMEM((tm, tf), BF16)]
    if long_seq:
        assert seq % tm == 0
        blocks_per_seq = seq // tm
        tail_shape = (m // tm, 2, 8, d_ff)
        tail_spec = pl.BlockSpec((1, 2, 8, tf), lambda i, j: (i, 0, 0, up_j(j)))
        scratch += [pltpu.VMEM((nj, 2, 8, tf), F32)]
    else:
        assert tm % seq == 0 and seq == 8 and seq >= hist
        blocks_per_seq = 0
        ns = tm // seq
        in_specs += [
            pl.BlockSpec((ns, hist, tf), lambda i, j: (i, 0, up_j(j))),
            pl.BlockSpec((ns, hist, tf), lambda i, j: (i, 0, up_j(j) + nj)),
        ]
        args += [state_conv, state_conv]
        tail_shape = (m // seq, 2, hist, d_ff)
        tail_spec = pl.BlockSpec((ns, 2, hist, tf), lambda i, j: (i, 0, 0, up_j(j)))
    y, tail = pl.pallas_call(
        functools.partial(_conv_ffn_kernel, seq=seq, blocks_per_seq=blocks_per_seq),
        grid=(m // tm, nj + 1),
        in_specs=in_specs,
        out_specs=[pl.BlockSpec((tm, d), lambda i, j: (i, 0)), tail_spec],
        out_shape=[jax.ShapeDtypeStruct((m, d), F32), jax.ShapeDtypeStruct(tail_shape, F32)],
        scratch_shapes=scratch,
        compiler_params=_params("arbitrary", "arbitrary"),
        name="conv_ffn",
    )(*args)
    if long_seq:
        tail = tail[blocks_per_seq - 1::blocks_per_seq, :, 8 - hist:, :]
    nseq = tail.shape[0]
    new_conv = jnp.swapaxes(tail, 1, 2).reshape(nseq, hist, 2 * d_ff)
    return y, new_conv


def _rotary_table(pos, rows):
    half = DK // 2
    theta = ROPE_BASE ** (-jnp.arange(half, dtype=F32) / half)
    ang = pos.astype(F32)[:, None] * theta[None, :]
    cos, sin = jnp.cos(ang), jnp.sin(ang)
    cs = jnp.concatenate([cos, cos, -sin, sin], axis=-1)
    reps = max(1, rows // cs.shape[0])
    return jnp.tile(cs, (reps, 1))


def kernel(x_prompt, x_sample, state_pool, state_ret, state_conv, g_pre_mix, w_in, w_pool, pool_scale, gn_gain,
           w_out, g_post_mix, g_pre_ffn, w_up, conv_w, conv_b, w_down, g_post_ffn):
    d = x_prompt.shape[-1]
    d_pool = state_pool.shape[-1]
    row = lambda v: v.reshape(1, -1)
    w_in_b, w_pool_b = w_in.astype(BF16), w_pool.astype(BF16)

    bp, lp, _ = x_prompt.shape
    xp = x_prompt.reshape(bp * lp, d)
    cs_p = _rotary_table(jnp.arange(lp, dtype=jnp.int32), IN_ROW_TILE)
    u_p, rest_p, w_up_b, w_down_b, w_out_b = _in_proj(
        xp, row(g_pre_mix), cs_p, w_in_b, d_pool=d_pool, rest_dtype=BF16, cast=(w_up, w_down, w_out))

    def dense_tail(x2, m2, state_conv_path, seq):
        x1, h2 = _out_proj(m2, w_out_b, x2, row(g_post_mix), row(g_pre_ffn))
        return _conv_ffn(h2, x1, w_up_b, conv_w, conv_b, w_down_b, row(g_post_ffn), state_conv_path, seq=seq)

    m_p, ret_p = _mixer_prompt(u_p, rest_p, w_pool_b, row(pool_scale), row(gn_gain), batch=bp, seq=lp)
    y_p, conv_p = dense_tail(xp, m_p, None, lp)
    pool_p = u_p.reshape(bp, lp, d_pool)[:, lp - (POOL_MAX - 1):]

    bs, ls, _ = x_sample.shape
    xs = x_sample.reshape(bs * ls, d)
    cs_s = _rotary_table(PAST_LEN + jnp.arange(ls, dtype=jnp.int32), IN_ROW_TILE)
    u_s, rest_s = _in_proj(xs, row(g_pre_mix), cs_s, w_in_b, d_pool=d_pool, rest_dtype=F32)
    m_s, ret_s = _mixer_sample(u_s, rest_s, state_pool, state_ret, w_pool_b, row(pool_scale), row(gn_gain),
                               batch=bs, seq=ls, pos0=PAST_LEN)
    y_s, conv_s = dense_tail(xs, m_s, state_conv, ls)
    pool_s = jnp.concatenate([state_pool, u_s.reshape(bs, ls, d_pool)], axis=1)[:, -(POOL_MAX - 1):]

    return (y_p.reshape(bp, lp, d), y_s.reshape(bs, ls, d), pool_p, ret_p.astype(x_prompt.dtype), conv_p,
            pool_s, ret_s.astype(x_sample.dtype), conv_s)
```
